```python
import jax, jax.numpy as jnp
from jax import lax
import numpy as np

D_MODEL = 4096
BATCH = 16
SEQ = 256
DEPTH = 4
DEC_BATCH = 8
DEC_SEQ = 4096
PAST_LEN = 256

GRID_W = 64
HEAD_DIM = 128
EPS = 1e-6
A_GROUPS = 4
A_CH = 128
A_WIDTH = A_GROUPS * A_CH
A_CHUNK = 128
B_HEADS = 4
B_DK = 128
B_DV = 128
B_KW = B_HEADS * B_DK
B_VW = B_HEADS * B_DV
B_CONV = 5
B_CHUNK = 64
C_HEADS = 8
C_KV = 2
C_QW = C_HEADS * HEAD_DIM
C_KVW = C_KV * HEAD_DIM
Q_BLOCK = 128
ROPE_THETA = 10000.0
N_BRANCH = 3
GATE_RANK = 256
FFN_BASE = D_MODEL // 2
FFN = -(-8 * FFN_BASE // (3 * 256)) * 256
IN_WIDTHS = (A_WIDTH, A_WIDTH, 2 * B_KW + B_VW, B_VW, 4 * B_HEADS, C_QW, 2 * C_KVW, GATE_RANK)
N_IN = sum(IN_WIDTHS)
IN_SPLITS = tuple(int(s) for s in np.cumsum(IN_WIDTHS)[:-1])

kernel_name = "hybrid_gmlp_deltanet_gqa_diffusion_step"


def rms_norm(x):
    xf = x.astype(jnp.float32)
    return (xf * lax.rsqrt(jnp.mean(xf * xf, axis=-1, keepdims=True) + EPS)).astype(x.dtype)


def l2_norm(x):
    xf = x.astype(jnp.float32)
    return (xf * lax.rsqrt(jnp.sum(xf * xf, axis=-1, keepdims=True) + EPS)).astype(x.dtype)


def modulate(x, shift, scale):
    return rms_norm(x) * (1.0 + scale) + shift


def ada_params(cond, w_ada, b_ada):
    m = jax.nn.silu(cond) @ w_ada + b_ada
    return m.reshape(cond.shape[0], 6, 1, D_MODEL)


def axial_rope_tables(n_tokens):
    rows = n_tokens // GRID_W
    row = jnp.repeat(jnp.arange(rows, dtype=jnp.float32), GRID_W)
    col = jnp.tile(jnp.arange(GRID_W, dtype=jnp.float32), rows)
    axis_dim = HEAD_DIM // 2
    inv = 1.0 / (ROPE_THETA ** (jnp.arange(0, axis_dim, 2, dtype=jnp.float32) / axis_dim))
    ang = jnp.stack([row[:, None] * inv, col[:, None] * inv], axis=1)
    return jnp.cos(ang), jnp.sin(ang)


def apply_axial_rope(x, cos, sin):
    B, L, H, D = x.shape
    xr = x.astype(jnp.float32).reshape(B, L, H, 2, 2, D // 4)
    x1, x2 = xr[..., 0, :], xr[..., 1, :]
    c = cos[None, :, None]
    s = sin[None, :, None]
    out = jnp.stack([x1 * c - x2 * s, x2 * c + x1 * s], axis=-2)
    return out.reshape(B, L, H, D).astype(x.dtype)


def gqa_attend(q, k, v):
    B, L, H, D = q.shape
    G = H // C_KV
    nb = L // Q_BLOCK
    qb = q.reshape(B, nb, Q_BLOCK, C_KV, G, D).transpose(1, 0, 2, 3, 4, 5)

    def one_block(q_blk):
        s = jnp.einsum('bqkgd,bskd->bkgqs', q_blk, k, preferred_element_type=jnp.float32)
        p = jax.nn.softmax(s, axis=-1).astype(v.dtype)
        return jnp.einsum('bkgqs,bskd->bqkgd', p, v)

    o = lax.map(one_block, qb)
    return o.transpose(1, 0, 2, 3, 4, 5).reshape(B, L, H * D)


def chunk_mlp_branch(u, v, v_gain, w_s, b_s):
    u = jax.nn.gelu(u)
    v = rms_norm(jax.nn.gelu(v)) * v_gain
    B, L, _ = v.shape
    vc = v.reshape(B, L // A_CHUNK, A_CHUNK, A_GROUPS, A_CH)
    mixed = jnp.einsum('gpq,bnqgc->bnpgc', w_s, vc) + b_s.T[None, None, :, :, None]
    return u * mixed.reshape(B, L, A_WIDTH)


def centred_depthwise_conv(x, w):
    K = w.shape[0]
    return lax.conv_general_dilated(x, w[:, None, :].astype(x.dtype), window_strides=(1,),
                                    padding=((K // 2, K // 2),), dimension_numbers=('NWC', 'WIO', 'NWC'),
                                    feature_group_count=x.shape[-1])


def gated_delta_chunked(q, k, v, g, beta, s0):
    B, L, H, DK = q.shape
    DV = v.shape[-1]
    C = B_CHUNK
    N = L // C

    def chunks(t):
        t = t.astype(jnp.float32).reshape((B, N, C, H) + t.shape[3:])
        return jnp.moveaxis(t, 3, 1)

    qc, kc, vc, gc, bc = chunks(q), chunks(k), chunks(v), chunks(g), chunks(beta)
    Gc = jnp.cumsum(gc, axis=-1)
    diff = Gc[..., :, None] - Gc[..., None, :]
    tri_incl = jnp.tril(jnp.ones((C, C), bool))
    tri_strict = jnp.tril(jnp.ones((C, C), bool), -1)
    decay = jnp.where(tri_incl, jnp.exp(jnp.where(tri_incl, diff, 0.0)), 0.0)
    kk = jnp.einsum('bhntd,bhnjd->bhntj', kc, kc)
    Lmat = jnp.where(tri_strict, bc[..., :, None] * kk * decay, 0.0)
    gam = jnp.exp(Gc)
    rhs = jnp.concatenate([(bc * gam)[..., None] * kc, bc[..., None] * vc], axis=-1)
    sol = lax.linalg.triangular_solve(jnp.eye(C, dtype=jnp.float32) + Lmat, rhs,
                                      left_side=True, lower=True, unit_diagonal=True)
    Wm, Uv = sol[..., :DK], sol[..., DK:]
    qk = jnp.einsum('bhntd,bhnjd->bhntj', qc, kc) * decay
    q_dec = gam[..., None] * qc
    k_dec = jnp.exp(Gc[..., -1:] - Gc)[..., None] * kc
    gam_end = gam[..., -1]
    xs = tuple(jnp.moveaxis(t, 2, 0) for t in (Wm, Uv, qk, q_dec, k_dec, gam_end))

    def step(S, inp):
        w_n, uv_n, qk_n, qd_n, kd_n, ge_n = inp
        U = uv_n - jnp.einsum('bhcd,bhde->bhce', w_n, S)
        O = jnp.einsum('bhcd,bhde->bhce', qd_n, S) + jnp.einsum('bhtj,bhje->bhte', qk_n, U)
        S = ge_n[..., None, None] * S + jnp.einsum('bhcd,bhce->bhde', kd_n, U)
        return S, O

    S_fin, O = lax.scan(step, s0.astype(jnp.float32), xs)
    O = O.transpose(1, 0, 3, 2, 4).reshape(B, L, H, DV)
    return O.astype(v.dtype), S_fin


def deltanet_branch(qkv, gate, ab, conv_w, a_log, dt_bias, o_gain, s0f, s0b):
    B, L, _ = qkv.shape
    qkv = jax.nn.silu(centred_depthwise_conv(qkv, conv_w))
    q, k, v = jnp.split(qkv, [B_KW, 2 * B_KW], axis=-1)
    q = l2_norm(q.reshape(B, L, B_HEADS, B_DK)) * (B_DK ** -0.5)
    k = l2_norm(k.reshape(B, L, B_HEADS, B_DK))
    v = v.reshape(B, L, B_HEADS, B_DV)
    a_f, a_b, b_f, b_b = jnp.split(ab.astype(jnp.float32), 4, axis=-1)
    a_log = a_log.astype(jnp.float32)
    dt_bias = dt_bias.astype(jnp.float32)
    g_f = -jnp.exp(a_log[0]) * jax.nn.softplus(a_f + dt_bias[0])
    g_b = -jnp.exp(a_log[1]) * jax.nn.softplus(a_b + dt_bias[1])
    o_f, s_f = gated_delta_chunked(q, k, v, g_f, jax.nn.sigmoid(b_f), s0f)
    rev = lambda t: jnp.flip(t, axis=1)
    o_b, s_b = gated_delta_chunked(rev(q), rev(k), rev(v), rev(g_b), rev(jax.nn.sigmoid(b_b)), s0b)
    o = o_f + rev(o_b)
    o = rms_norm(o) * o_gain * jax.nn.silu(gate.reshape(B, L, B_HEADS, B_DV))
    return o.reshape(B, L, B_VW), s_f, s_b


def attention_branch(q, k, v, q_gain, k_gain, ctx_kv):
    B, L, _ = q.shape
    q = rms_norm(q.reshape(B, L, C_HEADS, HEAD_DIM)) * q_gain
    k = rms_norm(k.reshape(B, L, C_KV, HEAD_DIM)) * k_gain
    v = v.reshape(B, L, C_KV, HEAD_DIM)
    if ctx_kv is None:
        k_all, v_all = k, v
        new_kv = (k, v)
    else:
        cos, sin = axial_rope_tables(L)
        q = apply_axial_rope(q, cos, sin)
        k = apply_axial_rope(k, cos, sin)
        k_all = jnp.concatenate([ctx_kv[0].astype(k.dtype), k], axis=1)
        v_all = jnp.concatenate([ctx_kv[1].astype(v.dtype), v], axis=1)
        new_kv = None
    o = gqa_attend(q * (HEAD_DIM ** -0.5), k_all, v_all)
    return o, new_kv


def trunk_layer(x, mod, w_in, a_v_gain, a_w_s, a_b_s, b_conv, b_a_log, b_dt_bias, b_o_gain,
                c_q_gain, c_k_gain, w_mg, w_br_a, w_br_b, w_br_c, w_o, w_gate, w_up, w_down, ctx):
    B, L, _ = x.shape
    h = modulate(x, mod[:, 0], mod[:, 1])
    proj = h @ w_in
    u_a, v_a, qkv_b, gate_b, ab_b, q_c, kv_c, gate_lr = jnp.split(proj, IN_SPLITS, axis=-1)
    k_c, v_c = jnp.split(kv_c, 2, axis=-1)
    if ctx is None:
        s0f = jnp.zeros((B, B_HEADS, B_DK, B_DV), jnp.float32)
        s0b = s0f
        ctx_kv = None
    else:
        s0f, s0b = ctx[2], ctx[3]
        ctx_kv = (ctx[0], ctx[1])
    y_a = chunk_mlp_branch(u_a, v_a, a_v_gain, a_w_s, a_b_s)
    y_b, s_f, s_b = deltanet_branch(qkv_b, gate_b, ab_b, b_conv, b_a_log, b_dt_bias, b_o_gain, s0f, s0b)
    y_c, new_kv = attention_branch(q_c, k_c, v_c, c_q_gain, c_k_gain, ctx_kv)
    g_logit = (gate_lr @ w_mg).astype(jnp.float32)
    g = jax.nn.sigmoid(g_logit).astype(x.dtype).reshape(B, L, N_BRANCH, D_MODEL)
    merged = g[:, :, 0] * (y_a @ w_br_a) + g[:, :, 1] * (y_b @ w_br_b) + g[:, :, 2] * (y_c @ w_br_c)
    x = x + mod[:, 2] * (merged @ w_o)
    h = modulate(x, mod[:, 3], mod[:, 4])
    x = x + mod[:, 5] * ((jax.nn.silu(h @ w_gate) * (h @ w_up)) @ w_down)
    if ctx is None:
        return x, (new_kv[0], new_kv[1], s_f, s_b)
    return x, None


def setup_inputs(seed: int = 0) -> dict:
    key = jax.random.key(seed)
    ks = jax.random.split(key, 32)
    D = D_MODEL

    def nrm(k, shape, s):
        return jax.random.normal(k, shape, jnp.float32) * s

    return dict(
        x_prompt=nrm(ks[0], (BATCH, SEQ, D), 1.0),
        x_sample=nrm(ks[1], (DEC_BATCH, DEC_SEQ, D), 1.0),
        cache_k=nrm(ks[2], (DEC_BATCH, DEPTH, PAST_LEN, C_KV, HEAD_DIM), 1.0),
        cache_v=nrm(ks[3], (DEC_BATCH, DEPTH, PAST_LEN, C_KV, HEAD_DIM), 1.0),
        state_fwd=nrm(ks[4], (DEC_BATCH, DEPTH, B_HEADS, B_DK, B_DV), B_DK ** -0.5),
        state_bwd=nrm(ks[5], (DEC_BATCH, DEPTH, B_HEADS, B_DK, B_DV), B_DK ** -0.5),
        c=nrm(ks[6], (DEC_BATCH, D), 1.0),
        c_ctx=nrm(ks[7], (D,), 1.0),
        w_ada=nrm(ks[8], (DEPTH, D, 6 * D), 0.5 * D ** -0.5),
        b_ada=nrm(ks[9], (DEPTH, 6 * D), 0.01),
        w_in=nrm(ks[10], (DEPTH, D, N_IN), D ** -0.5),
        a_v_gain=1.0 + nrm(ks[11], (DEPTH, A_WIDTH), 0.01),
        a_w_s=nrm(ks[12], (DEPTH, A_GROUPS, A_CHUNK, A_CHUNK), A_CHUNK ** -0.5),
        a_b_s=1.0 + nrm(ks[13], (DEPTH, A_GROUPS, A_CHUNK), 0.01),
        b_conv=nrm(ks[14], (DEPTH, B_CONV, 2 * B_KW + B_VW), B_CONV ** -0.5),
        b_a_log=jnp.log(jax.random.uniform(ks[15], (DEPTH, 2, B_HEADS), jnp.float32, 1.0, 16.0)),
        b_dt_bias=nrm(ks[16], (DEPTH, 2, B_HEADS), 0.1),
        b_o_gain=1.0 + nrm(ks[17], (DEPTH, B_DV), 0.01),
        c_q_gain=1.0 + nrm(ks[18], (DEPTH, HEAD_DIM), 0.01),
        c_k_gain=1.0 + nrm(ks[19], (DEPTH, HEAD_DIM), 0.01),
        w_mg=nrm(ks[20], (DEPTH, GATE_RANK, N_BRANCH * D), GATE_RANK ** -0.5),
        w_br_a=nrm(ks[21], (DEPTH, A_WIDTH, D), A_WIDTH ** -0.5),
        w_br_b=nrm(ks[22], (DEPTH, B_VW, D), B_VW ** -0.5),
        w_br_c=nrm(ks[23], (DEPTH, C_QW, D), C_QW ** -0.5),
        w_o=nrm(ks[24], (DEPTH, D, D), D ** -0.5),
        w_gate=nrm(ks[25], (DEPTH, D, FFN), D ** -0.5),
        w_up=nrm(ks[26], (DEPTH, D, FFN), D ** -0.5),
        w_down=nrm(ks[27], (DEPTH, FFN, D), FFN ** -0.5),
    )


def reference(x_prompt, x_sample, cache_k, cache_v, state_fwd, state_bwd, c, c_ctx,
              w_ada, b_ada, w_in, a_v_gain, a_w_s, a_b_s, b_conv, b_a_log, b_dt_bias, b_o_gain,
              c_q_gain, c_k_gain, w_mg, w_br_a, w_br_b, w_br_c, w_o, w_gate, w_up, w_down):
    yp = x_prompt
    ys = x_sample
    ks_, vs_, sfs_, sbs_ = [], [], [], []
    for l in range(DEPTH):
        lw = (w_in[l], a_v_gain[l], a_w_s[l], a_b_s[l], b_conv[l], b_a_log[l], b_dt_bias[l], b_o_gain[l],
              c_q_gain[l], c_k_gain[l], w_mg[l], w_br_a[l], w_br_b[l], w_br_c[l], w_o[l],
              w_gate[l], w_up[l], w_down[l])
        mod_p = ada_params(c_ctx[None, :], w_ada[l], b_ada[l])
        yp, (k_l, v_l, sf_l, sb_l) = trunk_layer(yp, mod_p, *lw, None)
        ks_.append(k_l)
        vs_.append(v_l)
        sfs_.append(sf_l)
        sbs_.append(sb_l)
        mod_s = ada_params(c, w_ada[l], b_ada[l])
        ys, _ = trunk_layer(ys, mod_s, *lw, (cache_k[:, l], cache_v[:, l], state_fwd[:, l], state_bwd[:, l]))
    new_cache_k = jnp.stack(ks_, axis=1)
    new_cache_v = jnp.stack(vs_, axis=1)
    new_state_fwd = jnp.stack(sfs_, axis=1)
    new_state_bwd = jnp.stack(sbs_, axis=1)
    return (yp, ys, new_cache_k, new_cache_v, new_state_fwd, new_state_bwd)
```

```python
import functools

import jax
import jax.numpy as jnp
import numpy as np
from jax import lax
from jax.experimental import pallas as pl
from jax.experimental.pallas import tpu as pltpu

F32 = jnp.float32
BF16 = jnp.bfloat16

EPS = 1e-6
GRID_W = 64
HEAD_DIM = 128
ROPE_THETA = 10000.0
A_GROUPS = 4
A_CH = 128
A_WIDTH = A_GROUPS * A_CH
A_CHUNK = 128
B_HEADS = 4
B_DK = 128
B_DV = 128
B_KW = B_HEADS * B_DK
B_VW = B_HEADS * B_DV
B_QKV = 2 * B_KW + B_VW
B_CONV = 5
B_CHUNK = 64
C_HEADS = 8
C_KV = 2
C_GROUP = C_HEADS // C_KV
C_QW = C_HEADS * HEAD_DIM
C_KVW = C_KV * HEAD_DIM
GATE_RANK = 256
N_BRANCH = 3
N_AB = 4 * B_HEADS

COL_U = 0
COL_V = COL_U + A_WIDTH
COL_QKV = COL_V + A_WIDTH
COL_GATE_B = COL_QKV + B_QKV
COL_Q = COL_GATE_B + B_VW
COL_K = COL_Q + C_QW
COL_VC = COL_K + C_KVW
COL_GLR = COL_VC + C_KVW
COL_AB = COL_GLR + GATE_RANK
AB_PAD = 256
AB_LANES = 128
N_PROJ = COL_AB + AB_PAD

VMEM_LIMIT_MB = 56


def _cparams(*sem):
    return pltpu.CompilerParams(dimension_semantics=sem, vmem_limit_bytes=VMEM_LIMIT_MB * 1024 * 1024)


def _sigmoid(x):
    return 1.0 / (1.0 + jnp.exp(-x))


def _silu(x):
    return x * _sigmoid(x)


def _gelu_tanh(x):
    return 0.5 * x * (1.0 + jnp.tanh(np.sqrt(2.0 / np.pi).astype(np.float32) * (x + 0.044715 * (x * x * x))))


def _dot(a, b):
    return jnp.dot(a, b, preferred_element_type=F32)


def _dot_nt(a, b):
    return lax.dot_general(a, b, (((1,), (1,)), ((), ())), preferred_element_type=F32)


def _dot_tn(a, b):
    return lax.dot_general(a, b, (((0,), (0,)), ((), ())), preferred_element_type=F32)


def _ada_kernel(c_ref, w_ref, b_ref, o_ref):
    c = c_ref[...]
    o_ref[...] = _dot(_silu(c).astype(BF16), w_ref[...].astype(BF16)) + b_ref[...]


def _ada_call(cond, w_ada, b_ada, tn=512):
    depth, d, n = w_ada.shape
    rows = cond.shape[0]
    return pl.pallas_call(
        _ada_kernel,
        grid=(depth, n // tn),
        in_specs=[
            pl.BlockSpec((rows, d), lambda l, j: (0, 0)),
            pl.BlockSpec((None, d, tn), lambda l, j: (l, 0, j)),
            pl.BlockSpec((None, 1, tn), lambda l, j: (l, 0, j)),
        ],
        out_specs=pl.BlockSpec((None, rows, tn), lambda l, j: (l, 0, j)),
        out_shape=jax.ShapeDtypeStruct((depth, rows, n), F32),
        compiler_params=_cparams("arbitrary", "arbitrary"),
        name="ada",
    )(cond, w_ada, b_ada.reshape(depth, 1, n))


def _norm_mod_rows(x_ref, shift_ref, scale_ref, h_ref, chunk=64):
    sc = 1.0 + scale_ref[...]
    sh = shift_ref[...]

    def body(r, carry):
        rows = pl.ds(pl.multiple_of(r * chunk, chunk), chunk)
        x = x_ref[rows, :]
        ms = jnp.mean(x * x, axis=-1, keepdims=True)
        h_ref[rows, :] = ((x * lax.rsqrt(ms + EPS)) * sc + sh).astype(BF16)
        return carry

    lax.fori_loop(0, x_ref.shape[0] // chunk, body, 0)


def _inproj_kernel(x_ref, shift_ref, scale_ref, w_ref, o_ref, h_ref):
    @pl.when(pl.program_id(1) == 0)
    def _():
        _norm_mod_rows(x_ref, shift_ref, scale_ref, h_ref)

    o_ref[...] = _dot(h_ref[...], w_ref[...])


def _mod_block(d, k, tm, group_rows):
    return pl.BlockSpec((None, None, 1, d), lambda i, j: ((i * tm) // group_rows, k, 0, 0))


def _inproj_call(x, mod, w, group_rows, tm=512, tn=1024):
    m, d = x.shape
    n = w.shape[1]
    return pl.pallas_call(
        _inproj_kernel,
        grid=(m // tm, n // tn),
        in_specs=[
            pl.BlockSpec((tm, d), lambda i, j: (i, 0)),
            _mod_block(d, 0, tm, group_rows),
            _mod_block(d, 1, tm, group_rows),
            pl.BlockSpec((d, tn), lambda i, j: (0, j)),
        ],
        out_specs=pl.BlockSpec((tm, tn), lambda i, j: (i, j)),
        out_shape=jax.ShapeDtypeStruct((m, n), F32),
        scratch_shapes=[pltpu.VMEM((tm, d), BF16)],
        compiler_params=_cparams("arbitrary", "arbitrary"),
        name="inproj",
    )(x, mod, mod, w)


def _merge_kernel(ya_ref, yb_ref, yc_ref, glr_ref, wg0_ref, wg1_ref, wg2_ref, wa_ref, wb_ref, wc_ref, o_ref):
    glr = glr_ref[...].astype(BF16)
    g0 = _sigmoid(_dot(glr, wg0_ref[...]))
    g1 = _sigmoid(_dot(glr, wg1_ref[...]))
    g2 = _sigmoid(_dot(glr, wg2_ref[...]))
    merged = (g0 * _dot(ya_ref[...], wa_ref[...]) + g1 * _dot(yb_ref[...], wb_ref[...])
              + g2 * _dot(yc_ref[...], wc_ref[...]))
    o_ref[...] = merged.astype(BF16)


def _merge_call(ya, yb, yc, proj, w_mg, w_a, w_b, w_c, tm=1024, tn=512):
    m = ya.shape[0]
    d = w_a.shape[1]
    nj = d // tn
    return pl.pallas_call(
        _merge_kernel,
        grid=(m // tm, nj),
        in_specs=[
            pl.BlockSpec((tm, A_WIDTH), lambda i, j: (i, 0)),
            pl.BlockSpec((tm, B_VW), lambda i, j: (i, 0)),
            pl.BlockSpec((tm, C_QW), lambda i, j: (i, 0)),
            pl.BlockSpec((tm, GATE_RANK), lambda i, j: (i, COL_GLR // GATE_RANK)),
            pl.BlockSpec((GATE_RANK, tn), lambda i, j: (0, j)),
            pl.BlockSpec((GATE_RANK, tn), lambda i, j: (0, nj + j)),
            pl.BlockSpec((GATE_RANK, tn), lambda i, j: (0, 2 * nj + j)),
            pl.BlockSpec((A_WIDTH, tn), lambda i, j: (0, j)),
            pl.BlockSpec((B_VW, tn), lambda i, j: (0, j)),
            pl.BlockSpec((C_QW, tn), lambda i, j: (0, j)),
        ],
        out_specs=pl.BlockSpec((tm, tn), lambda i, j: (i, j)),
        out_shape=jax.ShapeDtypeStruct((m, d), BF16),
        compiler_params=_cparams("arbitrary", "arbitrary"),
        name="merge",
    )(ya, yb, yc, proj, w_mg, w_mg, w_mg, w_a, w_b, w_c)


def _resid_kernel(a_ref, w_ref, x_ref, gate_ref, o_ref):
    o_ref[...] = x_ref[...] + gate_ref[...] * _dot(a_ref[...], w_ref[...])


def _resid_call(a, w, x, mod, gate_k, group_rows, tm=1024, tn=512, name="resid"):
    m, k = a.shape
    d = w.shape[1]
    return pl.pallas_call(
        _resid_kernel,
        grid=(m // tm, d // tn),
        in_specs=[
            pl.BlockSpec((tm, k), lambda i, j: (i, 0)),
            pl.BlockSpec((k, tn), lambda i, j: (0, j)),
            pl.BlockSpec((tm, tn), lambda i, j: (i, j)),
            pl.BlockSpec((None, None, 1, tn), lambda i, j: ((i * tm) // group_rows, gate_k, 0, j)),
        ],
        out_specs=pl.BlockSpec((tm, tn), lambda i, j: (i, j)),
        out_shape=jax.ShapeDtypeStruct((m, d), F32),
        compiler_params=_cparams("arbitrary", "arbitrary"),
        name=name,
    )(a, w, x, mod)


def _ffn_up_kernel(x_ref, shift_ref, scale_ref, wg_ref, wu_ref, o_ref, h_ref):
    @pl.when(pl.program_id(1) == 0)
    def _():
        _norm_mod_rows(x_ref, shift_ref, scale_ref, h_ref)

    h = h_ref[...]
    o_ref[...] = (_silu(_dot(h, wg_ref[...])) * _dot(h, wu_ref[...])).astype(BF16)


def _ffn_up_call(x, mod, w_gate, w_up, group_rows, tm=512, tn=512):
    m, d = x.shape
    n = w_gate.shape[1]
    return pl.pallas_call(
        _ffn_up_kernel,
        grid=(m // tm, n // tn),
        in_specs=[
            pl.BlockSpec((tm, d), lambda i, j: (i, 0)),
            _mod_block(d, 3, tm, group_rows),
            _mod_block(d, 4, tm, group_rows),
            pl.BlockSpec((d, tn), lambda i, j: (0, j)),
            pl.BlockSpec((d, tn), lambda i, j: (0, j)),
        ],
        out_specs=pl.BlockSpec((tm, tn), lambda i, j: (i, j)),
        out_shape=jax.ShapeDtypeStruct((m, n), BF16),
        scratch_shapes=[pltpu.VMEM((tm, d), BF16)],
        compiler_params=_cparams("arbitrary", "arbitrary"),
        name="ffn_up",
    )(x, mod, mod, w_gate, w_up)


def _gmlp_kernel(u_ref, v_ref, gain_ref, ws_ref, bs_ref, o_ref):
    gain = gain_ref[...]
    for c in range(u_ref.shape[0] // A_CHUNK):
        rows = slice(c * A_CHUNK, (c + 1) * A_CHUNK)
        v = _gelu_tanh(v_ref[rows, :])
        ms = jnp.mean(v * v, axis=-1, keepdims=True)
        v = ((v * lax.rsqrt(ms + EPS)) * gain).astype(BF16)
        u = _gelu_tanh(u_ref[rows, :])
        for g in range(A_GROUPS):
            cols = slice(g * A_CH, (g + 1) * A_CH)
            mixed = _dot(ws_ref[g], v[:, cols]) + bs_ref[:, g:g + 1]
            o_ref[rows, cols] = (u[:, cols] * mixed).astype(BF16)


def _gmlp_call(proj, v_gain, w_s, b_s, tm=512):
    m = proj.shape[0]
    return pl.pallas_call(
        _gmlp_kernel,
        grid=(m // tm,),
        in_specs=[
            pl.BlockSpec((tm, A_WIDTH), lambda i: (i, COL_U // A_WIDTH)),
            pl.BlockSpec((tm, A_WIDTH), lambda i: (i, COL_V // A_WIDTH)),
            pl.BlockSpec((1, A_WIDTH), lambda i: (0, 0)),
            pl.BlockSpec((A_GROUPS, A_CHUNK, A_CHUNK), lambda i: (0, 0, 0)),
            pl.BlockSpec((A_CHUNK, A_GROUPS), lambda i: (0, 0)),
        ],
        out_specs=pl.BlockSpec((tm, A_WIDTH), lambda i: (i, 0)),
        out_shape=jax.ShapeDtypeStruct((m, A_WIDTH), BF16),
        compiler_params=_cparams("arbitrary"),
        name="gmlp",
    )(proj, proj, v_gain.reshape(1, A_WIDTH), w_s.astype(BF16), b_s.T)


def _conv_prep_kernel(x_ref, w_ref, o_ref, pad_ref, *, seq, rows_per_step):
    halo = 8
    k_half = B_CONV // 2
    pad_ref[0:halo, :] = jnp.zeros((halo, B_DK), F32)
    pad_ref[halo + seq:2 * halo + seq, :] = jnp.zeros((halo, B_DK), F32)
    pad_ref[halo:halo + seq, :] = x_ref[...]
    kind = pl.program_id(1) // B_HEADS
    q_scale = jnp.where(kind == 0, B_DK ** -0.5, 1.0).astype(F32)
    w = w_ref[...]
    for r0 in range(0, seq, rows_per_step):
        acc = jnp.zeros((rows_per_step, B_DK), F32)
        for j in range(B_CONV):
            start = halo + r0 + j - k_half
            acc = acc + pad_ref[start:start + rows_per_step, :] * w[j:j + 1, :]
        y = _silu(acc)
        ss = jnp.sum(y * y, axis=-1, keepdims=True)
        factor = jnp.where(kind == 2, 1.0, lax.rsqrt(ss + EPS) * q_scale)
        o_ref[r0:r0 + rows_per_step, :] = y * factor


def _conv_prep_call(proj, conv_w, seq, n_seq, row_block0):
    m = proj.shape[0]
    nblk = B_QKV // B_DK
    kern = functools.partial(_conv_prep_kernel, seq=seq, rows_per_step=min(seq, 256))
    return pl.pallas_call(
        kern,
        grid=(n_seq, nblk),
        in_specs=[
            pl.BlockSpec((seq, B_DK), lambda s, c: (row_block0 + s, COL_QKV // B_DK + c)),
            pl.BlockSpec((B_CONV, B_DK), lambda s, c: (0, c)),
        ],
        out_specs=pl.BlockSpec((seq, B_DK), lambda s, c: (s, c)),
        out_shape=jax.ShapeDtypeStruct((n_seq * seq, B_QKV), F32),
        scratch_shapes=[pltpu.VMEM((seq + 16, B_DK), F32)],
        compiler_params=_cparams("arbitrary", "arbitrary"),
        name="conv_prep",
    )(proj, conv_w)


def _split3(x):
    hi = x.astype(BF16)
    r1 = x - hi.astype(F32)
    mid = r1.astype(BF16)
    lo = (r1 - mid.astype(F32)).astype(BF16)
    return hi, mid, lo


def _gate_prep_kernel(ab_ref, alog_ref, dt_ref, o_ref):
    t = ab_ref.shape[0]
    x = ab_ref[:, 0:AB_LANES]
    lane = lax.broadcasted_iota(jnp.int32, (t, AB_LANES), 1)
    z = x + dt_ref[...]
    softplus = jnp.maximum(z, 0.0) + jnp.log(1.0 + jnp.exp(-jnp.abs(z)))
    g = -jnp.exp(alog_ref[...]) * softplus
    beta = _sigmoid(x)
    ri = lax.broadcasted_iota(jnp.int32, (t, t), 0)
    ci = lax.broadcasted_iota(jnp.int32, (t, t), 1)
    same = (ri // B_CHUNK) == (ci // B_CHUNK)
    tril = jnp.where(same & (ci <= ri), 1.0, 0.0).astype(BF16)
    triu = jnp.where(same & (ci >= ri), 1.0, 0.0).astype(BF16)
    hi, mid, lo = _split3(g)
    cum_f = _dot(tril, hi) + _dot(tril, mid) + _dot(tril, lo)
    cum_b = _dot(triu, hi) + _dot(triu, mid) + _dot(triu, lo)
    res = jnp.where(lane < B_HEADS, cum_f, jnp.where(lane < 2 * B_HEADS, cum_b, beta))
    for h in range(B_HEADS):
        o_ref[h] = jnp.concatenate([res[:, h + B_HEADS * c:h + B_HEADS * c + 1] for c in range(4)], axis=1)


def _gate_prep_call(proj, a_log, dt_bias, tm=512):
    m = proj.shape[0]
    lane_pad = jnp.zeros((AB_LANES - 2 * B_HEADS,), F32)
    alog_row = jnp.concatenate([a_log.reshape(-1), lane_pad]).reshape(1, AB_LANES)
    dt_row = jnp.concatenate([dt_bias.reshape(-1), lane_pad]).reshape(1, AB_LANES)
    return pl.pallas_call(
        _gate_prep_kernel,
        grid=(m // tm,),
        in_specs=[
            pl.BlockSpec((tm, AB_PAD), lambda i: (i, COL_AB // AB_PAD)),
            pl.BlockSpec((1, AB_LANES), lambda i: (0, 0)),
            pl.BlockSpec((1, AB_LANES), lambda i: (0, 0)),
        ],
        out_specs=pl.BlockSpec((B_HEADS, tm, 4), lambda i: (0, i, 0)),
        out_shape=jax.ShapeDtypeStruct((B_HEADS, m, 4), F32),
        compiler_params=_cparams("arbitrary"),
        name="gate_prep",
    )(proj, alog_row, dt_row)


def _delta_kernel(*refs, seq, zero_init, emit_state):
    it = iter(refs)
    q_ref, k_ref, v_ref, gb_ref, gate_ref, og_ref = (next(it) for _ in range(6))
    s0_refs = None if zero_init else (next(it), next(it))
    y_ref = next(it)
    s_out = (next(it), next(it)) if emit_state else None
    scr = [tuple(next(it) for _ in range(6)) for _ in range(2)]

    c = B_CHUNK
    n_chunks = seq // c
    ri = lax.broadcasted_iota(jnp.int32, (c, c), 0)
    ci = lax.broadcasted_iota(jnp.int32, (c, c), 1)
    eye = jnp.where(ri == ci, 1.0, 0.0).astype(F32)

    def prep(n, d):
        wm_s, uv_s, qd_s, kd_s, qk_s, _ = scr[d]
        rows = pl.ds(pl.multiple_of(n * c, c), c)
        q = q_ref[rows, :]
        k = k_ref[rows, :]
        v = v_ref[rows, :]
        gbv = gb_ref[rows, :]
        gc = jnp.broadcast_to(gbv[:, d:d + 1], (c, B_DK))
        beta = jnp.broadcast_to(gbv[:, 2 + d:3 + d], (c, B_DK))
        gc_t = jnp.concatenate([gc, gc], axis=0).T[0:c, 0:c]
        diff = gc[:, 0:c] - gc_t
        incl = (ri >= ci) if d == 0 else (ri <= ci)
        strict = (ri > ci) if d == 0 else (ri < ci)
        decay = jnp.where(incl, jnp.exp(jnp.where(incl, diff, 0.0)), 0.0)
        kb = k.astype(BF16)
        kk = _dot_nt(kb, kb)
        qk = _dot_nt(q.astype(BF16), kb) * decay
        lmat = jnp.where(strict, beta[:, 0:c] * kk * decay, 0.0)
        gam = jnp.exp(gc)
        rhs = jnp.concatenate([(beta * gam) * k, beta * v], axis=1).astype(BF16)
        p = -lmat
        tinv = eye + p
        for _ in range(5):
            pb = p.astype(BF16)
            p = _dot(pb, pb)
            tinv = tinv + _dot(tinv.astype(BF16), p.astype(BF16))
        sol = _dot(tinv.astype(BF16), rhs)
        end_row = c - 1 if d == 0 else 0
        g_end = gc[end_row:end_row + 1, :]
        wm_s[rows, :] = sol[:, 0:B_DK].astype(BF16)
        uv_s[rows, :] = sol[:, B_DK:B_DK + B_DV]
        qd_s[rows, :] = (gam * q).astype(BF16)
        kd_s[rows, :] = (jnp.exp(g_end - gc) * k).astype(BF16)
        qk_s[rows, 0:c] = qk.astype(BF16)

    def prep_body(i, carry):
        prep(i, 0)
        prep(i, 1)
        return carry

    lax.fori_loop(0, n_chunks, prep_body, 0)

    def step(n, s, d):
        wm_s, uv_s, qd_s, kd_s, qk_s, o_s = scr[d]
        rows = pl.ds(pl.multiple_of(n * c, c), c)
        sb = s.astype(BF16)
        u = uv_s[rows, :] - _dot(wm_s[rows, :], sb)
        ub = u.astype(BF16)
        o_s[rows, :] = _dot(qd_s[rows, :], sb) + _dot(qk_s[rows, 0:c], ub)
        end_row = c - 1 if d == 0 else 0
        g_end = gb_ref[pl.ds(n * c + end_row, 1), :][:, d:d + 1]
        return jnp.exp(g_end) * s + _dot_tn(kd_s[rows, :], ub)

    def scan_body(i, carry):
        s_f, s_b = carry
        return step(i, s_f, 0), step(n_chunks - 1 - i, s_b, 1)

    if zero_init:
        init = (jnp.zeros((B_DK, B_DV), F32), jnp.zeros((B_DK, B_DV), F32))
    else:
        init = (s0_refs[0][...], s0_refs[1][...])
    s_f, s_b = lax.fori_loop(0, n_chunks, scan_body, init)
    if emit_state:
        s_out[0][...] = s_f
        s_out[1][...] = s_b

    og = og_ref[...]
    o_f, o_b = scr[0][5], scr[1][5]
    rstep = min(seq, 256)
    for r0 in range(0, seq, rstep):
        rows = slice(r0, r0 + rstep)
        o = o_f[rows, :] + o_b[rows, :]
        ms = jnp.mean(o * o, axis=-1, keepdims=True)
        y_ref[rows, :] = (((o * lax.rsqrt(ms + EPS)) * og) * _silu(gate_ref[rows, :])).astype(BF16)


def _delta_call(qkv, gb, proj, o_gain, states, seq, n_seq, row_block0, layer):
    zero_init = states is None
    emit_state = states is None
    kern = functools.partial(_delta_kernel, seq=seq, zero_init=zero_init, emit_state=emit_state)
    in_specs = [
        pl.BlockSpec((seq, B_DK), lambda s, h: (s, h)),
        pl.BlockSpec((seq, B_DK), lambda s, h: (s, B_HEADS + h)),
        pl.BlockSpec((seq, B_DV), lambda s, h: (s, 2 * B_HEADS + h)),
        pl.BlockSpec((None, seq, 4), lambda s, h: (h, row_block0 + s, 0)),
        pl.BlockSpec((seq, B_DV), lambda s, h: (row_block0 + s, COL_GATE_B // B_DV + h)),
        pl.BlockSpec((1, B_DV), lambda s, h: (0, 0)),
    ]
    args = [qkv, qkv, qkv, gb, proj, o_gain.reshape(1, B_DV)]
    if not zero_init:
        st_spec = pl.BlockSpec((None, None, None, B_DK, B_DV), lambda s, h: (s, layer, h, 0, 0))
        in_specs += [st_spec, st_spec]
        args += [states[0], states[1]]
    out_specs = [pl.BlockSpec((seq, B_DV), lambda s, h: (s, h))]
    out_shape = [jax.ShapeDtypeStruct((n_seq * seq, B_VW), BF16)]
    if emit_state:
        so_spec = pl.BlockSpec((None, None, B_DK, B_DV), lambda s, h: (s, h, 0, 0))
        out_specs += [so_spec, so_spec]
        out_shape += [jax.ShapeDtypeStruct((n_seq, B_HEADS, B_DK, B_DV), F32)] * 2
    per_dir = [
        pltpu.VMEM((seq, B_DK), BF16),
        pltpu.VMEM((seq, B_DV), F32),
        pltpu.VMEM((seq, B_DK), BF16),
        pltpu.VMEM((seq, B_DK), BF16),
        pltpu.VMEM((seq, 128), BF16),
        pltpu.VMEM((seq, B_DV), F32),
    ]
    return pl.pallas_call(
        kern,
        grid=(n_seq, B_HEADS),
        in_specs=in_specs,
        out_specs=out_specs,
        out_shape=out_shape,
        scratch_shapes=per_dir + per_dir,
        compiler_params=_cparams("arbitrary", "arbitrary"),
        name="delta",
    )(*args)


def _rope_tables(n_tokens):
    rows = n_tokens // GRID_W
    row = jnp.repeat(jnp.arange(rows, dtype=F32), GRID_W)
    col = jnp.tile(jnp.arange(GRID_W, dtype=F32), rows)
    axis_dim = HEAD_DIM // 2
    inv = 1.0 / (ROPE_THETA ** (jnp.arange(0, axis_dim, 2, dtype=F32) / axis_dim))
    ang_r = row[:, None] * inv
    ang_c = col[:, None] * inv
    cos = jnp.concatenate([jnp.cos(ang_r), jnp.cos(ang_r), jnp.cos(ang_c), jnp.cos(ang_c)], axis=1)
    sin = jnp.concatenate([-jnp.sin(ang_r), jnp.sin(ang_r), -jnp.sin(ang_c), jnp.sin(ang_c)], axis=1)
    return cos, sin


def _rope(x, cos, sin):
    quarter = HEAD_DIM // 4
    lane = lax.broadcasted_iota(jnp.int32, x.shape, 1)
    first = (lane % (2 * quarter)) < quarter
    partner = jnp.where(first, pltpu.roll(x, HEAD_DIM - quarter, 1), pltpu.roll(x, quarter, 1))
    return x * cos + partner * sin


def _rms_gain(x, gain):
    ms = jnp.mean(x * x, axis=-1, keepdims=True)
    return (x * lax.rsqrt(ms + EPS)) * gain


def _attn_kernel(*refs, seq, n_ctx, rope, tq, emit_k):
    it = iter(refs)
    q_ref, k_ref, v_ref = next(it), next(it), next(it)
    ck_ref, cv_ref = (next(it), next(it)) if n_ctx else (None, None)
    cos_ref, sin_ref = (next(it), next(it)) if rope else (None, None)
    qg_ref, kg_ref = next(it), next(it)
    o_ref = next(it)
    knew_ref = next(it) if emit_k else None
    kbuf, vbuf, sbuf = next(it), next(it), next(it)

    n_keys = seq + n_ctx
    qb = pl.program_id(2)
    kstep = min(seq, 512)

    @pl.when(qb == 0)
    def _():
        kg = kg_ref[...]
        for r0 in range(0, seq, kstep):
            rows = slice(r0, r0 + kstep)
            kn = _rms_gain(k_ref[rows, :], kg)
            if emit_k:
                knew_ref[rows, :] = kn
            if rope:
                kn = _rope(kn, cos_ref[rows, :], sin_ref[rows, :])
            kbuf[rows, :] = kn.astype(BF16)
            vbuf[rows, 0:HEAD_DIM] = v_ref[rows, :].astype(BF16)
        if n_ctx:
            kbuf[seq:n_keys, :] = ck_ref[...].astype(BF16)
            vbuf[seq:n_keys, 0:HEAD_DIM] = cv_ref[...].astype(BF16)
        vbuf[:, HEAD_DIM:2 * HEAD_DIM] = jnp.ones((n_keys, HEAD_DIM), BF16)

    qg = qg_ref[...]
    q_rows = pl.ds(pl.multiple_of(qb * tq, tq), tq)
    heads = []
    for g in range(C_GROUP):
        qh = _rms_gain(q_ref[:, g * HEAD_DIM:(g + 1) * HEAD_DIM], qg)
        if rope:
            qh = _rope(qh, cos_ref[q_rows, :], sin_ref[q_rows, :])
        heads.append((qh * (HEAD_DIM ** -0.5)).astype(BF16))
    q4 = jnp.concatenate(heads, axis=0)
    m_rows = C_GROUP * tq

    blocks = []
    start = 0
    while start < n_keys:
        size = min(512, n_keys - start)
        blocks.append((start, size))
        start += size

    m_lane = jnp.full((m_rows, 128), -jnp.inf, F32)
    for (b0, bs) in blocks:
        s = _dot_nt(q4, kbuf[b0:b0 + bs, :])
        sbuf[:, b0:b0 + bs] = s
        for j in range(bs // 128):
            m_lane = jnp.maximum(m_lane, s[:, j * 128:(j + 1) * 128])
    m = jnp.max(m_lane, axis=-1, keepdims=True)

    acc = jnp.zeros((m_rows, 2 * HEAD_DIM), F32)
    for (b0, bs) in blocks:
        p = jnp.exp(sbuf[:, b0:b0 + bs] - m).astype(BF16)
        acc = acc + _dot(p, vbuf[b0:b0 + bs, :])
    o = acc[:, 0:HEAD_DIM] / acc[:, HEAD_DIM:2 * HEAD_DIM]
    for g in range(C_GROUP):
        o_ref[:, g * HEAD_DIM:(g + 1) * HEAD_DIM] = o[g * tq:(g + 1) * tq, :].astype(BF16)


def _attn_call(proj, q_gain, k_gain, seq, n_seq, row_block0, tq, cache=None, layer=0, tables=None):
    n_ctx = 0 if cache is None else cache[0].shape[2]
    rope = tables is not None
    emit_k = cache is None
    qw = C_GROUP * HEAD_DIM
    nq = seq // tq
    kern = functools.partial(_attn_kernel, seq=seq, n_ctx=n_ctx, rope=rope, tq=tq, emit_k=emit_k)
    in_specs = [
        pl.BlockSpec((tq, qw), lambda b, h, i: ((row_block0 + b) * nq + i, COL_Q // qw + h)),
        pl.BlockSpec((seq, HEAD_DIM), lambda b, h, i: (row_block0 + b, COL_K // HEAD_DIM + h)),
        pl.BlockSpec((seq, HEAD_DIM), lambda b, h, i: (row_block0 + b, COL_VC // HEAD_DIM + h)),
    ]
    args = [proj, proj, proj]
    if n_ctx:
        c_spec = pl.BlockSpec((None, None, n_ctx, HEAD_DIM), lambda b, h, i: (b, layer, 0, h))
        in_specs += [c_spec, c_spec]
        args += [cache[0], cache[1]]
    if rope:
        t_spec = pl.BlockSpec((seq, HEAD_DIM), lambda b, h, i: (0, 0))
        in_specs += [t_spec, t_spec]
        args += [tables[0], tables[1]]
    g_spec = pl.BlockSpec((1, HEAD_DIM), lambda b, h, i: (0, 0))
    in_specs += [g_spec, g_spec]
    args += [q_gain.reshape(1, HEAD_DIM), k_gain.reshape(1, HEAD_DIM)]
    out_specs = [pl.BlockSpec((tq, qw), lambda b, h, i: (b * nq + i, h))]
    out_shape = [jax.ShapeDtypeStruct((n_seq * seq, C_QW), BF16)]
    if emit_k:
        out_specs.append(pl.BlockSpec((seq, HEAD_DIM), lambda b, h, i: (b, h)))
        out_shape.append(jax.ShapeDtypeStruct((n_seq * seq, C_KVW), F32))
    n_keys = seq + n_ctx
    return pl.pallas_call(
        kern,
        grid=(n_seq, C_KV, nq),
        in_specs=in_specs,
        out_specs=out_specs,
        out_shape=out_shape,
        scratch_shapes=[
            pltpu.VMEM((n_keys, HEAD_DIM), BF16),
            pltpu.VMEM((n_keys, 2 * HEAD_DIM), BF16),
            pltpu.VMEM((C_GROUP * tq, n_keys), F32),
        ],
        compiler_params=_cparams("arbitrary", "arbitrary", "arbitrary"),
        name="attn",
    )(*args)


def kernel(x_prompt, x_sample, cache_k, cache_v, state_fwd, state_bwd, c, c_ctx, w_ada, b_ada, w_in, a_v_gain,
           a_w_s, a_b_s, b_conv, b_a_log, b_dt_bias, b_o_gain, c_q_gain, c_k_gain, w_mg, w_br_a, w_br_b, w_br_c,
           w_o, w_gate, w_up, w_down):
    batch, seq, d = x_prompt.shape
    dec_batch, dec_seq, _ = x_sample.shape
    depth = w_in.shape[0]
    n_ctx_rows = batch * seq
    group_rows = dec_seq
    assert n_ctx_rows == group_rows, "context tokens must fill exactly one modulation group"
    n_groups = 1 + dec_batch
    past_len = cache_k.shape[2]

    x = jnp.concatenate([x_prompt.reshape(n_ctx_rows, d), x_sample.reshape(dec_batch * dec_seq, d)], axis=0)

    cond_rows = -(-n_groups // 8) * 8
    cond = jnp.concatenate([c_ctx[None, :], c, jnp.zeros((cond_rows - n_groups, d), F32)], axis=0)
    mod_all = _ada_call(cond, w_ada, b_ada)[:, :n_groups].reshape(depth, n_groups, 6, 1, d)

    n_in = w_in.shape[2]
    ab0 = COL_GATE_B + B_VW
    cache_k2 = cache_k.reshape(dec_batch, depth, past_len, C_KVW)
    cache_v2 = cache_v.reshape(dec_batch, depth, past_len, C_KVW)
    tables = _rope_tables(dec_seq)

    new_k, new_v, new_sf, new_sb = [], [], [], []
    for l in range(depth):
        mod = mod_all[l]
        w_in_l = jnp.concatenate(
            [w_in[l][:, :ab0], w_in[l][:, ab0 + N_AB:n_in], w_in[l][:, ab0:ab0 + N_AB],
             jnp.zeros((d, AB_PAD - N_AB), F32)], axis=1).astype(BF16)
        proj = _inproj_call(x, mod, w_in_l, group_rows)

        y_a = _gmlp_call(proj, a_v_gain[l], a_w_s[l], a_b_s[l])

        gb = _gate_prep_call(proj, b_a_log[l], b_dt_bias[l])
        qkv_ctx = _conv_prep_call(proj, b_conv[l], seq, batch, 0)
        qkv_lat = _conv_prep_call(proj, b_conv[l], dec_seq, dec_batch, 1)
        yb_ctx, sf, sb = _delta_call(qkv_ctx, gb, proj, b_o_gain[l], None, seq, batch, 0, l)
        (yb_lat,) = _delta_call(qkv_lat, gb, proj, b_o_gain[l], (state_fwd, state_bwd), dec_seq, dec_batch, 1, l)
        y_b = jnp.concatenate([yb_ctx, yb_lat], axis=0)
        new_sf.append(sf)
        new_sb.append(sb)

        yc_ctx, k_new = _attn_call(proj, c_q_gain[l], c_k_gain[l], seq, batch, 0, tq=seq)
        (yc_lat,) = _attn_call(proj, c_q_gain[l], c_k_gain[l], dec_seq, dec_batch, 1, tq=128,
                               cache=(cache_k2, cache_v2), layer=l, tables=tables)
        y_c = jnp.concatenate([yc_ctx, yc_lat], axis=0)
        new_k.append(k_new.reshape(batch, seq, C_KV, HEAD_DIM))
        new_v.append(proj[:n_ctx_rows, COL_VC:COL_VC + C_KVW].reshape(batch, seq, C_KV, HEAD_DIM))

        merged = _merge_call(y_a, y_b, y_c, proj, w_mg[l].astype(BF16), w_br_a[l].astype(BF16),
                             w_br_b[l].astype(BF16), w_br_c[l].astype(BF16))
        x = _resid_call(merged, w_o[l].astype(BF16), x, mod, 2, group_rows, name="out_proj")
        act = _ffn_up_call(x, mod, w_gate[l].astype(BF16), w_up[l].astype(BF16), group_rows)
        x = _resid_call(act, w_down[l].astype(BF16), x, mod, 5, group_rows, name="ffn_down")

    y_prompt = x[:n_ctx_rows].reshape(batch, seq, d)
    y_sample = x[n_ctx_rows:].reshape(dec_batch, dec_seq, d)
    return (y_prompt, y_sample, jnp.stack(new_k, axis=1), jnp.stack(new_v, axis=1),
            jnp.stack(new_sf, axis=1), jnp.stack(new_sb, axis=1))
```

```python
import functools

import jax
import jax.numpy as jnp
import numpy as np
from jax import lax
from jax.experimental import pallas as pl
from jax.experimental.pallas import tpu as pltpu

F32 = jnp.float32
BF16 = jnp.bfloat16

EPS = 1e-6
GRID_W = 64
HEAD_DIM = 128
ROPE_THETA = 10000.0
A_GROUPS = 4
A_CH = 128
A_WIDTH = A_GROUPS * A_CH
A_CHUNK = 128
B_HEADS = 4
B_DK = 128
B_DV = 128
B_KW = B_HEADS * B_DK
B_VW = B_HEADS * B_DV
B_QKV = 2 * B_KW + B_VW
B_CONV = 5
B_CHUNK = 64
B_CHUNK_LOG2 = 6
C_HEADS = 8
C_KV = 2
C_GROUP = C_HEADS // C_KV
C_QW = C_HEADS * HEAD_DIM
C_KVW = C_KV * HEAD_DIM
GATE_RANK = 256
N_BRANCH = 3
N_AB = 4 * B_HEADS

COL_U = 0
COL_V = COL_U + A_WIDTH
COL_QKV = COL_V + A_WIDTH
COL_GATE_B = COL_QKV + B_QKV
COL_Q = COL_GATE_B + B_VW
COL_K = COL_Q + C_QW
COL_VC = COL_K + C_KVW
COL_GLR = COL_VC + C_KVW
COL_AB = COL_GLR + GATE_RANK
AB_PAD = 256
AB_LANES = 128
N_PROJ = COL_AB + AB_PAD

MXU_DIM = 256
DELTA_GROUP = MXU_DIM // B_CHUNK
DELTA_ROWS = DELTA_GROUP * B_CHUNK
KEY_BLOCK = 512
VMEM_LIMIT_MB = 56

TILES_INPROJ = ((512, 1024), (512, 512), (512, 1280), (1024, 256))
TILES_FFN_UP = ((512, 512), (512, 256), (1024, 256), (512, 512))
TILES_FFN_DOWN = ((1024, 512), (512, 1024), (1024, 256), (512, 512))
TILES_OUT_PROJ = ((1024, 512), (512, 1024), (1024, 1024), (2048, 256))
TILES_MERGE = ((1024, 512), (512, 512), (1024, 1024), (2048, 512))


def _cparams(*sem):
    return pltpu.CompilerParams(dimension_semantics=sem, vmem_limit_bytes=VMEM_LIMIT_MB * 1024 * 1024)


def _sigmoid(x):
    return 1.0 / (1.0 + jnp.exp(-x))


def _silu(x):
    return x * _sigmoid(x)


def _gelu_tanh(x):
    return 0.5 * x * (1.0 + jnp.tanh(np.sqrt(2.0 / np.pi).astype(np.float32) * (x + 0.044715 * (x * x * x))))


def _dot(a, b):
    return jnp.dot(a, b, preferred_element_type=F32)


def _dot_nt(a, b):
    return lax.dot_general(a, b, (((1,), (1,)), ((), ())), preferred_element_type=F32)


def _ada_kernel(c_ref, w_ref, b_ref, o_ref):
    c = c_ref[...]
    o_ref[...] = _dot(_silu(c).astype(BF16), w_ref[...].astype(BF16)) + b_ref[...]


def _ada_call(cond, w_ada, b_ada, tn=512):
    depth, d, n = w_ada.shape
    rows = cond.shape[0]
    return pl.pallas_call(
        _ada_kernel,
        grid=(depth, n // tn),
        in_specs=[
            pl.BlockSpec((rows, d), lambda l, j: (0, 0)),
            pl.BlockSpec((None, d, tn), lambda l, j: (l, 0, j)),
            pl.BlockSpec((None, 1, tn), lambda l, j: (l, 0, j)),
        ],
        out_specs=pl.BlockSpec((None, rows, tn), lambda l, j: (l, 0, j)),
        out_shape=jax.ShapeDtypeStruct((depth, rows, n), F32),
        compiler_params=_cparams("arbitrary", "arbitrary"),
        name="ada",
    )(cond, w_ada, b_ada.reshape(depth, 1, n))


def _norm_mod_rows(x_ref, shift_ref, scale_ref, h_ref, chunk=64):
    sc = 1.0 + scale_ref[...]
    sh = shift_ref[...]

    def body(r, carry):
        rows = pl.ds(pl.multiple_of(r * chunk, chunk), chunk)
        x = x_ref[rows, :]
        ms = jnp.mean(x * x, axis=-1, keepdims=True)
        h_ref[rows, :] = ((x * lax.rsqrt(ms + EPS)) * sc + sh).astype(BF16)
        return carry

    lax.fori_loop(0, x_ref.shape[0] // chunk, body, 0)


def _inproj_kernel(x_ref, shift_ref, scale_ref, w_ref, o_ref, h_ref):
    @pl.when(pl.program_id(1) == 0)
    def _():
        _norm_mod_rows(x_ref, shift_ref, scale_ref, h_ref)

    o_ref[...] = _dot(h_ref[...], w_ref[...])


def _mod_block(d, k, tm, group_rows):
    return pl.BlockSpec((None, None, 1, d), lambda i, j: ((i * tm) // group_rows, k, 0, 0))


def _inproj_call(x, mod, w, group_rows, tm, tn):
    m, d = x.shape
    n = w.shape[1]
    return pl.pallas_call(
        _inproj_kernel,
        grid=(m // tm, n // tn),
        in_specs=[
            pl.BlockSpec((tm, d), lambda i, j: (i, 0)),
            _mod_block(d, 0, tm, group_rows),
            _mod_block(d, 1, tm, group_rows),
            pl.BlockSpec((d, tn), lambda i, j: (0, j)),
        ],
        out_specs=pl.BlockSpec((tm, tn), lambda i, j: (i, j)),
        out_shape=jax.ShapeDtypeStruct((m, n), F32),
        scratch_shapes=[pltpu.VMEM((tm, d), BF16)],
        compiler_params=_cparams("arbitrary", "arbitrary"),
        name="inproj",
    )(x, mod, mod, w)


def _merge_kernel(ya_ref, ybc_ref, ybl_ref, ycc_ref, ycl_ref, glr_ref, wg0_ref, wg1_ref, wg2_ref, wa_ref, wb_ref,
                  wc_ref, o_ref, *, n_ctx_tiles):
    def compute(yb_ref, yc_ref):
        glr = glr_ref[...].astype(BF16)
        g0 = _sigmoid(_dot(glr, wg0_ref[...]))
        g1 = _sigmoid(_dot(glr, wg1_ref[...]))
        g2 = _sigmoid(_dot(glr, wg2_ref[...]))
        merged = (g0 * _dot(ya_ref[...], wa_ref[...]) + g1 * _dot(yb_ref[...], wb_ref[...])
                  + g2 * _dot(yc_ref[...], wc_ref[...]))
        o_ref[...] = merged.astype(BF16)

    is_ctx = pl.program_id(0) < n_ctx_tiles

    @pl.when(is_ctx)
    def _():
        compute(ybc_ref, ycc_ref)

    @pl.when(jnp.logical_not(is_ctx))
    def _():
        compute(ybl_ref, ycl_ref)


def _merge_call(ya, yb_ctx, yb_lat, yc_ctx, yc_lat, proj, w_mg, w_a, w_b, w_c, tm, tn):
    m = ya.shape[0]
    d = w_a.shape[1]
    nj = d // tn
    n_ctx_tiles = yb_ctx.shape[0] // tm
    ctx_map = lambda i, j: (jnp.minimum(i, n_ctx_tiles - 1), 0)
    lat_map = lambda i, j: (jnp.maximum(i - n_ctx_tiles, 0), 0)
    return pl.pallas_call(
        functools.partial(_merge_kernel, n_ctx_tiles=n_ctx_tiles),
        grid=(m // tm, nj),
        in_specs=[
            pl.BlockSpec((tm, A_WIDTH), lambda i, j: (i, 0)),
            pl.BlockSpec((tm, B_VW), ctx_map),
            pl.BlockSpec((tm, B_VW), lat_map),
            pl.BlockSpec((tm, C_QW), ctx_map),
            pl.BlockSpec((tm, C_QW), lat_map),
            pl.BlockSpec((tm, GATE_RANK), lambda i, j: (i, COL_GLR // GATE_RANK)),
            pl.BlockSpec((GATE_RANK, tn), lambda i, j: (0, j)),
            pl.BlockSpec((GATE_RANK, tn), lambda i, j: (0, nj + j)),
            pl.BlockSpec((GATE_RANK, tn), lambda i, j: (0, 2 * nj + j)),
            pl.BlockSpec((A_WIDTH, tn), lambda i, j: (0, j)),
            pl.BlockSpec((B_VW, tn), lambda i, j: (0, j)),
            pl.BlockSpec((C_QW, tn), lambda i, j: (0, j)),
        ],
        out_specs=pl.BlockSpec((tm, tn), lambda i, j: (i, j)),
        out_shape=jax.ShapeDtypeStruct((m, d), BF16),
        compiler_params=_cparams("arbitrary", "arbitrary"),
        name="merge",
    )(ya, yb_ctx, yb_lat, yc_ctx, yc_lat, proj, w_mg, w_mg, w_mg, w_a, w_b, w_c)


def _resid_kernel(a_ref, w_ref, x_ref, gate_ref, o_ref):
    o_ref[...] = x_ref[...] + gate_ref[...] * _dot(a_ref[...], w_ref[...])


def _resid_call(a, w, x, mod, gate_k, group_rows, tm, tn, name, row0=0, n_rows=None):
    m, k = a.shape
    d = w.shape[1]
    n_rows = m if n_rows is None else n_rows
    i0 = row0 // tm
    return pl.pallas_call(
        _resid_kernel,
        grid=(n_rows // tm, d // tn),
        in_specs=[
            pl.BlockSpec((tm, k), lambda i, j: (i0 + i, 0)),
            pl.BlockSpec((k, tn), lambda i, j: (0, j)),
            pl.BlockSpec((tm, tn), lambda i, j: (i0 + i, j)),
            pl.BlockSpec((None, None, 1, tn), lambda i, j: (((i0 + i) * tm) // group_rows, gate_k, 0, j)),
        ],
        out_specs=pl.BlockSpec((tm, tn), lambda i, j: (i, j)),
        out_shape=jax.ShapeDtypeStruct((n_rows, d), F32),
        compiler_params=_cparams("arbitrary", "arbitrary"),
        name=name,
    )(a, w, x, mod)


def _ffn_up_kernel(x_ref, shift_ref, scale_ref, wg_ref, wu_ref, o_ref, h_ref):
    @pl.when(pl.program_id(1) == 0)
    def _():
        _norm_mod_rows(x_ref, shift_ref, scale_ref, h_ref)

    h = h_ref[...]
    o_ref[...] = (_silu(_dot(h, wg_ref[...])) * _dot(h, wu_ref[...])).astype(BF16)


def _ffn_up_call(x, mod, w_gate, w_up, group_rows, tm, tn):
    m, d = x.shape
    n = w_gate.shape[1]
    return pl.pallas_call(
        _ffn_up_kernel,
        grid=(m // tm, n // tn),
        in_specs=[
            pl.BlockSpec((tm, d), lambda i, j: (i, 0)),
            _mod_block(d, 3, tm, group_rows),
            _mod_block(d, 4, tm, group_rows),
            pl.BlockSpec((d, tn), lambda i, j: (0, j)),
            pl.BlockSpec((d, tn), lambda i, j: (0, j)),
        ],
        out_specs=pl.BlockSpec((tm, tn), lambda i, j: (i, j)),
        out_shape=jax.ShapeDtypeStruct((m, n), BF16),
        scratch_shapes=[pltpu.VMEM((tm, d), BF16)],
        compiler_params=_cparams("arbitrary", "arbitrary"),
        name="ffn_up",
    )(x, mod, mod, w_gate, w_up)


def _gmlp_kernel(u_ref, v_ref, gain_ref, ws_ref, bs_ref, o_ref):
    gain = gain_ref[...]
    for c in range(u_ref.shape[0] // A_CHUNK):
        rows = slice(c * A_CHUNK, (c + 1) * A_CHUNK)
        v = _gelu_tanh(v_ref[rows, :])
        ms = jnp.mean(v * v, axis=-1, keepdims=True)
        v = ((v * lax.rsqrt(ms + EPS)) * gain).astype(BF16)
        u = _gelu_tanh(u_ref[rows, :])
        for g in range(A_GROUPS):
            cols = slice(g * A_CH, (g + 1) * A_CH)
            mixed = _dot(ws_ref[g], v[:, cols]) + bs_ref[:, g:g + 1]
            o_ref[rows, cols] = (u[:, cols] * mixed).astype(BF16)


def _gmlp_call(proj, v_gain, w_s, b_s, tm=512):
    m = proj.shape[0]
    return pl.pallas_call(
        _gmlp_kernel,
        grid=(m // tm,),
        in_specs=[
            pl.BlockSpec((tm, A_WIDTH), lambda i: (i, COL_U // A_WIDTH)),
            pl.BlockSpec((tm, A_WIDTH), lambda i: (i, COL_V // A_WIDTH)),
            pl.BlockSpec((1, A_WIDTH), lambda i: (0, 0)),
            pl.BlockSpec((A_GROUPS, A_CHUNK, A_CHUNK), lambda i: (0, 0, 0)),
            pl.BlockSpec((A_CHUNK, A_GROUPS), lambda i: (0, 0)),
        ],
        out_specs=pl.BlockSpec((tm, A_WIDTH), lambda i: (i, 0)),
        out_shape=jax.ShapeDtypeStruct((m, A_WIDTH), BF16),
        compiler_params=_cparams("arbitrary"),
        name="gmlp",
    )(proj, proj, v_gain.reshape(1, A_WIDTH), w_s.astype(BF16), b_s.T)


def _conv_prep_kernel(x_ref, w_ref, o_ref, pad_ref, *, seq, rows_per_step):
    halo = 8
    k_half = B_CONV // 2
    pad_ref[0:halo, :] = jnp.zeros((halo, B_DK), F32)
    pad_ref[halo + seq:2 * halo + seq, :] = jnp.zeros((halo, B_DK), F32)
    pad_ref[halo:halo + seq, :] = x_ref[...]
    kind = pl.program_id(1) // B_HEADS
    q_scale = jnp.where(kind == 0, B_DK ** -0.5, 1.0).astype(F32)
    w = w_ref[...]
    for r0 in range(0, seq, rows_per_step):
        acc = jnp.zeros((rows_per_step, B_DK), F32)
        for j in range(B_CONV):
            start = halo + r0 + j - k_half
            acc = acc + pad_ref[start:start + rows_per_step, :] * w[j:j + 1, :]
        y = _silu(acc)
        ss = jnp.sum(y * y, axis=-1, keepdims=True)
        factor = jnp.where(kind == 2, 1.0, lax.rsqrt(ss + EPS) * q_scale)
        o_ref[r0:r0 + rows_per_step, :] = y * factor


def _conv_prep_call(proj, conv_w, seq, n_seq, row_block0):
    nblk = B_QKV // B_DK
    kern = functools.partial(_conv_prep_kernel, seq=seq, rows_per_step=min(seq, 256))
    return pl.pallas_call(
        kern,
        grid=(n_seq, nblk),
        in_specs=[
            pl.BlockSpec((seq, B_DK), lambda s, c: (row_block0 + s, COL_QKV // B_DK + c)),
            pl.BlockSpec((B_CONV, B_DK), lambda s, c: (0, c)),
        ],
        out_specs=pl.BlockSpec((seq, B_DK), lambda s, c: (s, c)),
        out_shape=jax.ShapeDtypeStruct((n_seq * seq, B_QKV), F32),
        scratch_shapes=[pltpu.VMEM((seq + 16, B_DK), F32)],
        compiler_params=_cparams("arbitrary", "arbitrary"),
        name="conv_prep",
    )(proj, conv_w)


def _split3(x):
    hi = x.astype(BF16)
    r1 = x - hi.astype(F32)
    mid = r1.astype(BF16)
    lo = (r1 - mid.astype(F32)).astype(BF16)
    return hi, mid, lo


def _gate_prep_kernel(ab_ref, alog_ref, dt_ref, o_ref):
    t = ab_ref.shape[0]
    x = ab_ref[:, 0:AB_LANES]
    lane = lax.broadcasted_iota(jnp.int32, (t, AB_LANES), 1)
    z = x + dt_ref[...]
    softplus = jnp.maximum(z, 0.0) + jnp.log(1.0 + jnp.exp(-jnp.abs(z)))
    g = -jnp.exp(alog_ref[...]) * softplus
    beta = _sigmoid(x)
    ri = lax.broadcasted_iota(jnp.int32, (t, t), 0)
    ci = lax.broadcasted_iota(jnp.int32, (t, t), 1)
    same = (ri // B_CHUNK) == (ci // B_CHUNK)
    tril = jnp.where(same & (ci <= ri), 1.0, 0.0).astype(BF16)
    triu = jnp.where(same & (ci >= ri), 1.0, 0.0).astype(BF16)
    hi, mid, lo = _split3(g)
    cum_f = _dot(tril, hi) + _dot(tril, mid) + _dot(tril, lo)
    cum_b = _dot(triu, hi) + _dot(triu, mid) + _dot(triu, lo)
    res = jnp.where(lane < B_HEADS, cum_f, jnp.where(lane < 2 * B_HEADS, cum_b, beta))
    for h in range(B_HEADS):
        o_ref[h] = jnp.concatenate([res[:, h + B_HEADS * c:h + B_HEADS * c + 1] for c in range(4)], axis=1)


def _gate_prep_call(proj, a_log, dt_bias, tm=512):
    m = proj.shape[0]
    lane_pad = jnp.zeros((AB_LANES - 2 * B_HEADS,), F32)
    alog_row = jnp.concatenate([a_log.reshape(-1), lane_pad]).reshape(1, AB_LANES)
    dt_row = jnp.concatenate([dt_bias.reshape(-1), lane_pad]).reshape(1, AB_LANES)
    return pl.pallas_call(
        _gate_prep_kernel,
        grid=(m // tm,),
        in_specs=[
            pl.BlockSpec((tm, AB_PAD), lambda i: (i, COL_AB // AB_PAD)),
            pl.BlockSpec((1, AB_LANES), lambda i: (0, 0)),
            pl.BlockSpec((1, AB_LANES), lambda i: (0, 0)),
        ],
        out_specs=pl.BlockSpec((B_HEADS, tm, 4), lambda i: (0, i, 0)),
        out_shape=jax.ShapeDtypeStruct((B_HEADS, m, 4), F32),
        compiler_params=_cparams("arbitrary"),
        name="gate_prep",
    )(proj, alog_row, dt_row)


def _delta_kernel(*refs, seq, zero_init, emit_state):
    it = iter(refs)
    q_ref, k_ref, v_ref, gb_ref, gbr_ref, gate_ref, og_ref = (next(it) for _ in range(7))
    s0_refs = None if zero_init else (next(it), next(it))
    y_ref = next(it)
    s_out = (next(it), next(it)) if emit_state else None
    scr = [tuple(next(it) for _ in range(3)) for _ in range(2)]

    c = B_CHUNK
    r = DELTA_ROWS
    n_chunks = seq // c
    ri = lax.broadcasted_iota(jnp.int32, (r, r), 0)
    ci = lax.broadcasted_iota(jnp.int32, (r, r), 1)
    same = (ri >> B_CHUNK_LOG2) == (ci >> B_CHUNK_LOG2)
    masks = ((same & (ri >= ci), same & (ri > ci)), (same & (ri <= ci), same & (ri < ci)))
    col_chunk = lax.broadcasted_iota(jnp.int32, (B_DK, r), 1) >> B_CHUNK_LOG2

    def prep_group(gi, carry):
        rows = pl.ds(pl.multiple_of(gi * r, r), r)
        q = q_ref[rows, :]
        k = k_ref[rows, :]
        v = v_ref[rows, :]
        gbv = gb_ref[rows, :]
        gbr = gbr_ref[gi]
        kb = k.astype(BF16)
        kq = _dot_nt(jnp.concatenate([kb, q.astype(BF16)], axis=0), kb)
        kk = kq[0:r, :]
        qk_raw = kq[r:2 * r, :]
        for d in range(2):
            incl, strict = masks[d]
            gc_col = gbv[:, d:d + 1]
            beta_col = gbv[:, 2 + d:3 + d]
            gc = jnp.broadcast_to(gc_col, (r, B_DK))
            beta = jnp.broadcast_to(beta_col, (r, B_DK))
            diff = jnp.broadcast_to(gc_col, (r, r)) - jnp.broadcast_to(gbr[d:d + 1, :], (r, r))
            decay = jnp.where(incl, jnp.exp(jnp.where(incl, diff, 0.0)), 0.0)
            lmat = jnp.where(strict, jnp.broadcast_to(beta_col, (r, r)) * kk * decay, 0.0)
            gam = jnp.exp(gc)
            sol = jnp.concatenate([(beta * gam) * k, beta * v], axis=1)
            pb = (-lmat).astype(BF16)
            sol = sol + _dot(pb, sol.astype(BF16))
            for _ in range(B_CHUNK_LOG2 - 1):
                pb = _dot(pb, pb).astype(BF16)
                sol = sol + _dot(pb, sol.astype(BF16))
            sol = sol.astype(BF16)
            end_row = c - 1 if d == 0 else 0
            g_end = jnp.concatenate(
                [jnp.broadcast_to(gc[j * c + end_row:j * c + end_row + 1, :], (c, B_DK)) for j in range(DELTA_GROUP)],
                axis=0)
            kd = jnp.exp(g_end - gc) * k
            kd_t = kd.T
            lhs = jnp.concatenate([qk_raw * decay] + [jnp.where(col_chunk == j, kd_t, 0.0)
                                                      for j in range(DELTA_GROUP)], axis=0)
            res = _dot(lhs.astype(BF16), sol)
            qo = res[0:r, :]
            ab = res[r:, :]
            q_prime = (gam * q - qo[:, 0:B_DK]).astype(BF16)
            aq_s, b_s, o_s = scr[d]
            for j in range(DELTA_GROUP):
                n = gi * DELTA_GROUP + j
                aq_s[n, 0:B_DK, :] = (-ab[j * B_DK:(j + 1) * B_DK, 0:B_DK]).astype(BF16)
                aq_s[n, B_DK:B_DK + c, :] = q_prime[j * c:(j + 1) * c, :]
                b_s[n] = ab[j * B_DK:(j + 1) * B_DK, B_DK:B_DK + B_DV]
            o_s[rows, :] = qo[:, B_DK:B_DK + B_DV]
        return carry

    n_groups = seq // r
    lax.fori_loop(0, n_groups, prep_group, 0, unroll=2 if n_groups % 2 == 0 else 1)

    def step(n, s, d):
        aq_s, b_s, o_s = scr[d]
        rows = pl.ds(pl.multiple_of(n * c, c), c)
        res = _dot(aq_s[n], s.astype(BF16))
        o_s[rows, :] = o_s[rows, :] + res[B_DK:B_DK + c, :]
        end_row = c - 1 if d == 0 else 0
        g_end = gb_ref[pl.ds(n * c + end_row, 1), :][:, d:d + 1]
        return jnp.exp(g_end) * s + res[0:B_DK, :] + b_s[n]

    def scan_body(i, carry):
        s_f, s_b = carry
        return step(i, s_f, 0), step(n_chunks - 1 - i, s_b, 1)

    if zero_init:
        init = (jnp.zeros((B_DK, B_DV), F32), jnp.zeros((B_DK, B_DV), F32))
    else:
        init = (s0_refs[0][...], s0_refs[1][...])
    s_f, s_b = lax.fori_loop(0, n_chunks, scan_body, init, unroll=2)
    if emit_state:
        s_out[0][...] = s_f
        s_out[1][...] = s_b

    og = og_ref[...]
    o_f, o_b = scr[0][2], scr[1][2]
    rstep = min(seq, 256)
    for r0 in range(0, seq, rstep):
        rows = slice(r0, r0 + rstep)
        o = o_f[rows, :] + o_b[rows, :]
        ms = jnp.mean(o * o, axis=-1, keepdims=True)
        y_ref[rows, :] = (((o * lax.rsqrt(ms + EPS)) * og) * _silu(gate_ref[rows, :])).astype(BF16)


def _delta_call(qkv, gb, gbr, proj, o_gain, states, seq, n_seq, row_block0, layer):
    zero_init = states is None
    emit_state = states is None
    kern = functools.partial(_delta_kernel, seq=seq, zero_init=zero_init, emit_state=emit_state)
    n_grp = seq // DELTA_ROWS
    in_specs = [
        pl.BlockSpec((seq, B_DK), lambda s, h: (s, h)),
        pl.BlockSpec((seq, B_DK), lambda s, h: (s, B_HEADS + h)),
        pl.BlockSpec((seq, B_DV), lambda s, h: (s, 2 * B_HEADS + h)),
        pl.BlockSpec((None, seq, 4), lambda s, h: (h, row_block0 + s, 0)),
        pl.BlockSpec((None, n_grp, 4, DELTA_ROWS), lambda s, h: (h, row_block0 + s, 0, 0)),
        pl.BlockSpec((seq, B_DV), lambda s, h: (row_block0 + s, COL_GATE_B // B_DV + h)),
        pl.BlockSpec((1, B_DV), lambda s, h: (0, 0)),
    ]
    args = [qkv, qkv, qkv, gb, gbr, proj, o_gain.reshape(1, B_DV)]
    if not zero_init:
        st_spec = pl.BlockSpec((None, None, None, B_DK, B_DV), lambda s, h: (s, layer, h, 0, 0))
        in_specs += [st_spec, st_spec]
        args += [states[0], states[1]]
    out_specs = [pl.BlockSpec((seq, B_DV), lambda s, h: (s, h))]
    out_shape = [jax.ShapeDtypeStruct((n_seq * seq, B_VW), BF16)]
    if emit_state:
        so_spec = pl.BlockSpec((None, None, B_DK, B_DV), lambda s, h: (s, h, 0, 0))
        out_specs += [so_spec, so_spec]
        out_shape += [jax.ShapeDtypeStruct((n_seq, B_HEADS, B_DK, B_DV), F32)] * 2
    n_chunks = seq // B_CHUNK
    per_dir = [
        pltpu.VMEM((n_chunks, B_DK + B_CHUNK, B_DK), BF16),
        pltpu.VMEM((n_chunks, B_DK, B_DV), F32),
        pltpu.VMEM((seq, B_DV), F32),
    ]
    return pl.pallas_call(
        kern,
        grid=(n_seq, B_HEADS),
        in_specs=in_specs,
        out_specs=out_specs,
        out_shape=out_shape,
        scratch_shapes=per_dir + per_dir,
        compiler_params=_cparams("arbitrary", "arbitrary"),
        name="delta",
    )(*args)


def _rope_tables(n_tokens):
    rows = n_tokens // GRID_W
    row = jnp.repeat(jnp.arange(rows, dtype=F32), GRID_W)
    col = jnp.tile(jnp.arange(GRID_W, dtype=F32), rows)
    axis_dim = HEAD_DIM // 2
    inv = 1.0 / (ROPE_THETA ** (jnp.arange(0, axis_dim, 2, dtype=F32) / axis_dim))
    ang_r = row[:, None] * inv
    ang_c = col[:, None] * inv
    cos = jnp.concatenate([jnp.cos(ang_r), jnp.cos(ang_r), jnp.cos(ang_c), jnp.cos(ang_c)], axis=1)
    sin = jnp.concatenate([-jnp.sin(ang_r), jnp.sin(ang_r), -jnp.sin(ang_c), jnp.sin(ang_c)], axis=1)
    return cos, sin


def _rope(x, cos, sin):
    quarter = HEAD_DIM // 4
    lane = lax.broadcasted_iota(jnp.int32, x.shape, 1)
    first = (lane % (2 * quarter)) < quarter
    partner = jnp.where(first, pltpu.roll(x, HEAD_DIM - quarter, 1), pltpu.roll(x, quarter, 1))
    return x * cos + partner * sin


def _rms_gain(x, gain):
    ms = jnp.mean(x * x, axis=-1, keepdims=True)
    return (x * lax.rsqrt(ms + EPS)) * gain


def _attn_kernel(*refs, seq, n_ctx, rope, tq, emit_k):
    it = iter(refs)
    q_ref, k_ref, v_ref = next(it), next(it), next(it)
    ck_ref, cv_ref = (next(it), next(it)) if n_ctx else (None, None)
    cos_ref, sin_ref = (next(it), next(it)) if rope else (None, None)
    qg_ref, kg_ref = next(it), next(it)
    o_ref = next(it)
    knew_ref = next(it) if emit_k else None
    kt_buf, vbuf, sbuf = next(it), next(it), next(it)

    n_keys = seq + n_ctx
    qb = pl.program_id(2)
    kstep = min(seq, 512)

    @pl.when(qb == 0)
    def _():
        kg = kg_ref[...]
        for r0 in range(0, seq, kstep):
            rows = slice(r0, r0 + kstep)
            kn = _rms_gain(k_ref[rows, :], kg)
            if emit_k:
                knew_ref[rows, :] = kn
            if rope:
                kn = _rope(kn, cos_ref[rows, :], sin_ref[rows, :])
            kt_buf[:, rows] = kn.T.astype(BF16)
            vbuf[rows, 0:HEAD_DIM] = v_ref[rows, :].astype(BF16)
        if n_ctx:
            kt_buf[:, seq:n_keys] = ck_ref[...].T.astype(BF16)
            vbuf[seq:n_keys, 0:HEAD_DIM] = cv_ref[...].astype(BF16)
        vbuf[:, HEAD_DIM:2 * HEAD_DIM] = jnp.ones((n_keys, HEAD_DIM), BF16)

    qg = qg_ref[...]
    q_rows = pl.ds(pl.multiple_of(qb * tq, tq), tq)
    heads = []
    for g in range(C_GROUP):
        qh = _rms_gain(q_ref[:, g * HEAD_DIM:(g + 1) * HEAD_DIM], qg)
        if rope:
            qh = _rope(qh, cos_ref[q_rows, :], sin_ref[q_rows, :])
        heads.append((qh * (HEAD_DIM ** -0.5)).astype(BF16))
    q4 = jnp.concatenate(heads, axis=0)
    m_rows = C_GROUP * tq

    blocks = []
    start = 0
    while start < n_keys:
        size = min(KEY_BLOCK, n_keys - start)
        blocks.append((start, size))
        start += size

    m_lane = jnp.full((m_rows, 128), -jnp.inf, F32)
    for (b0, bs) in blocks:
        s = _dot(q4, kt_buf[:, b0:b0 + bs])
        sbuf[:, b0:b0 + bs] = s
        for j in range(bs // 128):
            m_lane = jnp.maximum(m_lane, s[:, j * 128:(j + 1) * 128])
    m = jnp.max(m_lane, axis=-1, keepdims=True)

    acc = jnp.zeros((m_rows, 2 * HEAD_DIM), F32)
    for (b0, bs) in blocks:
        p = jnp.exp(sbuf[:, b0:b0 + bs] - m).astype(BF16)
        acc = acc + _dot(p, vbuf[b0:b0 + bs, :])
    o = acc[:, 0:HEAD_DIM] / acc[:, HEAD_DIM:2 * HEAD_DIM]
    for g in range(C_GROUP):
        o_ref[:, g * HEAD_DIM:(g + 1) * HEAD_DIM] = o[g * tq:(g + 1) * tq, :].astype(BF16)


def _attn_call(proj, q_gain, k_gain, seq, n_seq, row_block0, tq, cache=None, layer=0, tables=None):
    n_ctx = 0 if cache is None else cache[0].shape[2]
    rope = tables is not None
    emit_k = cache is None
    qw = C_GROUP * HEAD_DIM
    nq = seq // tq
    kern = functools.partial(_attn_kernel, seq=seq, n_ctx=n_ctx, rope=rope, tq=tq, emit_k=emit_k)
    in_specs = [
        pl.BlockSpec((tq, qw), lambda b, h, i: ((row_block0 + b) * nq + i, COL_Q // qw + h)),
        pl.BlockSpec((seq, HEAD_DIM), lambda b, h, i: (row_block0 + b, COL_K // HEAD_DIM + h)),
        pl.BlockSpec((seq, HEAD_DIM), lambda b, h, i: (row_block0 + b, COL_VC // HEAD_DIM + h)),
    ]
    args = [proj, proj, proj]
    if n_ctx:
        c_spec = pl.BlockSpec((None, None, n_ctx, HEAD_DIM), lambda b, h, i: (b, layer, 0, h))
        in_specs += [c_spec, c_spec]
        args += [cache[0], cache[1]]
    if rope:
        t_spec = pl.BlockSpec((seq, HEAD_DIM), lambda b, h, i: (0, 0))
        in_specs += [t_spec, t_spec]
        args += [tables[0], tables[1]]
    g_spec = pl.BlockSpec((1, HEAD_DIM), lambda b, h, i: (0, 0))
    in_specs += [g_spec, g_spec]
    args += [q_gain.reshape(1, HEAD_DIM), k_gain.reshape(1, HEAD_DIM)]
    out_specs = [pl.BlockSpec((tq, qw), lambda b, h, i: (b * nq + i, h))]
    out_shape = [jax.ShapeDtypeStruct((n_seq * seq, C_QW), BF16)]
    if emit_k:
        out_specs.append(pl.BlockSpec((seq, HEAD_DIM), lambda b, h, i: (b, h)))
        out_shape.append(jax.ShapeDtypeStruct((n_seq * seq, C_KVW), F32))
    n_keys = seq + n_ctx
    return pl.pallas_call(
        kern,
        grid=(n_seq, C_KV, nq),
        in_specs=in_specs,
        out_specs=out_specs,
        out_shape=out_shape,
        scratch_shapes=[
            pltpu.VMEM((HEAD_DIM, n_keys), BF16),
            pltpu.VMEM((n_keys, 2 * HEAD_DIM), BF16),
            pltpu.VMEM((C_GROUP * tq, n_keys), F32),
        ],
        compiler_params=_cparams("arbitrary", "arbitrary", "arbitrary"),
        name="attn",
    )(*args)


def _tile(table, layer, m_limit, n_limit):
    tm, tn = table[layer % len(table)]
    return min(tm, m_limit), min(tn, n_limit)


def kernel(x_prompt, x_sample, cache_k, cache_v, state_fwd, state_bwd, c, c_ctx, w_ada, b_ada, w_in, a_v_gain,
           a_w_s, a_b_s, b_conv, b_a_log, b_dt_bias, b_o_gain, c_q_gain, c_k_gain, w_mg, w_br_a, w_br_b, w_br_c,
           w_o, w_gate, w_up, w_down):
    batch, seq, d = x_prompt.shape
    dec_batch, dec_seq, _ = x_sample.shape
    depth = w_in.shape[0]
    n_ctx_rows = batch * seq
    n_lat_rows = dec_batch * dec_seq
    group_rows = dec_seq
    assert n_ctx_rows == group_rows, "context tokens must fill exactly one modulation group"
    n_groups = 1 + dec_batch
    n_tokens = n_ctx_rows + n_lat_rows
    past_len = cache_k.shape[2]

    x = jnp.concatenate([x_prompt.reshape(n_ctx_rows, d), x_sample.reshape(n_lat_rows, d)], axis=0)

    cond_rows = -(-n_groups // 8) * 8
    cond = jnp.concatenate([c_ctx[None, :], c, jnp.zeros((cond_rows - n_groups, d), F32)], axis=0)
    mod_all = _ada_call(cond, w_ada, b_ada)[:, :n_groups].reshape(depth, n_groups, 6, 1, d)

    n_in = w_in.shape[2]
    ab0 = COL_GATE_B + B_VW
    cache_k2 = cache_k.reshape(dec_batch, depth, past_len, C_KVW)
    cache_v2 = cache_v.reshape(dec_batch, depth, past_len, C_KVW)
    tables = _rope_tables(dec_seq)

    new_k, new_v, new_sf, new_sb = [], [], [], []
    y_prompt = y_sample = None
    for l in range(depth):
        mod = mod_all[l]
        w_in_l = jnp.concatenate(
            [w_in[l][:, :ab0], w_in[l][:, ab0 + N_AB:n_in], w_in[l][:, ab0:ab0 + N_AB],
             jnp.zeros((d, AB_PAD - N_AB), F32)], axis=1).astype(BF16)
        proj = _inproj_call(x, mod, w_in_l, group_rows, *_tile(TILES_INPROJ, l, group_rows, N_PROJ))

        y_a = _gmlp_call(proj, a_v_gain[l], a_w_s[l], a_b_s[l])

        gb = _gate_prep_call(proj, b_a_log[l], b_dt_bias[l])
        gbr = gb.reshape(B_HEADS, n_tokens // DELTA_ROWS, DELTA_ROWS, 4).transpose(0, 1, 3, 2)
        qkv_ctx = _conv_prep_call(proj, b_conv[l], seq, batch, 0)
        qkv_lat = _conv_prep_call(proj, b_conv[l], dec_seq, dec_batch, 1)
        yb_ctx, sf, sb = _delta_call(qkv_ctx, gb, gbr, proj, b_o_gain[l], None, seq, batch, 0, l)
        (yb_lat,) = _delta_call(qkv_lat, gb, gbr, proj, b_o_gain[l], (state_fwd, state_bwd), dec_seq, dec_batch,
                                1, l)
        new_sf.append(sf)
        new_sb.append(sb)

        yc_ctx, k_new = _attn_call(proj, c_q_gain[l], c_k_gain[l], seq, batch, 0, tq=seq)
        (yc_lat,) = _attn_call(proj, c_q_gain[l], c_k_gain[l], dec_seq, dec_batch, 1, tq=256,
                               cache=(cache_k2, cache_v2), layer=l, tables=tables)
        new_k.append(k_new.reshape(batch, seq, C_KV, HEAD_DIM))
        new_v.append(proj[:n_ctx_rows, COL_VC:COL_VC + C_KVW].reshape(batch, seq, C_KV, HEAD_DIM))

        merged = _merge_call(y_a, yb_ctx, yb_lat, yc_ctx, yc_lat, proj, w_mg[l].astype(BF16),
                             w_br_a[l].astype(BF16), w_br_b[l].astype(BF16), w_br_c[l].astype(BF16),
                             *_tile(TILES_MERGE, l, group_rows, d))
        x = _resid_call(merged, w_o[l].astype(BF16), x, mod, 2, group_rows, *_tile(TILES_OUT_PROJ, l, group_rows, d),
                        name="out_proj")
        act = _ffn_up_call(x, mod, w_gate[l].astype(BF16), w_up[l].astype(BF16), group_rows,
                           *_tile(TILES_FFN_UP, l, group_rows, w_gate.shape[2]))
        w_down_l = w_down[l].astype(BF16)
        tm, tn = _tile(TILES_FFN_DOWN, l, group_rows, d)
        if l + 1 < depth:
            x = _resid_call(act, w_down_l, x, mod, 5, group_rows, tm, tn, name="ffn_down")
        else:
            y_prompt = _resid_call(act, w_down_l, x, mod, 5, group_rows, tm, tn, name="ffn_down_ctx",
                                   row0=0, n_rows=n_ctx_rows)
            y_sample = _resid_call(act, w_down_l, x, mod, 5, group_rows, tm, tn, name="ffn_down_lat",
                                   row0=n_ctx_rows, n_rows=n_lat_rows)

    return (y_prompt.reshape(batch, seq, d), y_sample.reshape(dec_batch, dec_seq, d),
            jnp.stack(new_k, axis=1), jnp.stack(new_v, axis=1), jnp.stack(new_sf, axis=1), jnp.stack(new_sb, axis=1))
```

```python
import functools

import jax
import jax.numpy as jnp
import numpy as np
from jax import lax
from jax.experimental import pallas as pl
from jax.experimental.pallas import tpu as pltpu

F32 = jnp.float32
BF16 = jnp.bfloat16

EPS = 1e-6
GRID_W = 64
HEAD_DIM = 128
ROPE_THETA = 10000.0
A_GROUPS = 4
A_CH = 128
A_WIDTH = A_GROUPS * A_CH
A_CHUNK = 128
B_HEADS = 4
B_DK = 128
B_DV = 128
B_KW = B_HEADS * B_DK
B_VW = B_HEADS * B_DV
B_QKV = 2 * B_KW + B_VW
B_CONV = 5
B_CHUNK = 64
B_CHUNK_LOG2 = 6
C_HEADS = 8
C_KV = 2
C_GROUP = C_HEADS // C_KV
C_QW = C_HEADS * HEAD_DIM
C_KVW = C_KV * HEAD_DIM
GATE_RANK = 256
N_BRANCH = 3
N_AB = 4 * B_HEADS

COL_U = 0
COL_V = COL_U + A_WIDTH
COL_QKV = COL_V + A_WIDTH
COL_GATE_B = COL_QKV + B_QKV
COL_Q = COL_GATE_B + B_VW
COL_K = COL_Q + C_QW
COL_VC = COL_K + C_KVW
COL_GLR = COL_VC + C_KVW
COL_AB = COL_GLR + GATE_RANK
AB_PAD = 256
AB_LANES = 128
N_PROJ = COL_AB + AB_PAD

MXU_DIM = 256
DELTA_GROUP = MXU_DIM // B_CHUNK
DELTA_ROWS = DELTA_GROUP * B_CHUNK
KEY_BLOCK = 512
VMEM_LIMIT_MB = 56

TILES_INPROJ = ((512, 1280),)
TILES_FFN_UP = ((512, 512),)
TILES_FFN_DOWN = ((1024, 512),)
TILES_OUT_PROJ = ((1024, 1024),)
TILES_MERGE = ((1024, 1024),)


def _cparams(*sem):
    return pltpu.CompilerParams(dimension_semantics=sem, vmem_limit_bytes=VMEM_LIMIT_MB * 1024 * 1024)


def _sigmoid(x):
    return 1.0 / (1.0 + jnp.exp(-x))


def _silu(x):
    return x * _sigmoid(x)


def _gelu_tanh(x):
    return 0.5 * x * (1.0 + jnp.tanh(np.sqrt(2.0 / np.pi).astype(np.float32) * (x + 0.044715 * (x * x * x))))


def _dot(a, b):
    return jnp.dot(a, b, preferred_element_type=F32)


def _dot_nt(a, b):
    return lax.dot_general(a, b, (((1,), (1,)), ((), ())), preferred_element_type=F32)


def _ada_kernel(c_ref, w_ref, b_ref, o_ref):
    c = c_ref[...]
    o_ref[...] = _dot(_silu(c).astype(BF16), w_ref[...].astype(BF16)) + b_ref[...]


def _ada_call(cond, w_ada, b_ada, tn=512):
    depth, d, n = w_ada.shape
    rows = cond.shape[0]
    return pl.pallas_call(
        _ada_kernel,
        grid=(depth, n // tn),
        in_specs=[
            pl.BlockSpec((rows, d), lambda l, j: (0, 0)),
            pl.BlockSpec((None, d, tn), lambda l, j: (l, 0, j)),
            pl.BlockSpec((None, 1, tn), lambda l, j: (l, 0, j)),
        ],
        out_specs=pl.BlockSpec((None, rows, tn), lambda l, j: (l, 0, j)),
        out_shape=jax.ShapeDtypeStruct((depth, rows, n), F32),
        compiler_params=_cparams("arbitrary", "arbitrary"),
        name="ada",
    )(cond, w_ada, b_ada.reshape(depth, 1, n))


def _norm_mod_rows(x_ref, shift_ref, scale_ref, h_ref, chunk=64):
    sc = 1.0 + scale_ref[...]
    sh = shift_ref[...]

    def body(r, carry):
        rows = pl.ds(pl.multiple_of(r * chunk, chunk), chunk)
        x = x_ref[rows, :]
        ms = jnp.mean(x * x, axis=-1, keepdims=True)
        h_ref[rows, :] = ((x * lax.rsqrt(ms + EPS)) * sc + sh).astype(BF16)
        return carry

    lax.fori_loop(0, x_ref.shape[0] // chunk, body, 0)


def _inproj_kernel(x_ref, shift_ref, scale_ref, w_ref, o_ref, h_ref):
    @pl.when(pl.program_id(1) == 0)
    def _():
        _norm_mod_rows(x_ref, shift_ref, scale_ref, h_ref)

    o_ref[...] = _dot(h_ref[...], w_ref[...])


def _mod_block(d, k, tm, group_rows):
    return pl.BlockSpec((None, None, 1, d), lambda i, j: ((i * tm) // group_rows, k, 0, 0))


def _inproj_call(x, mod, w, group_rows, tm, tn):
    m, d = x.shape
    n = w.shape[1]
    return pl.pallas_call(
        _inproj_kernel,
        grid=(m // tm, n // tn),
        in_specs=[
            pl.BlockSpec((tm, d), lambda i, j: (i, 0)),
            _mod_block(d, 0, tm, group_rows),
            _mod_block(d, 1, tm, group_rows),
            pl.BlockSpec((d, tn), lambda i, j: (0, j)),
        ],
        out_specs=pl.BlockSpec((tm, tn), lambda i, j: (i, j)),
        out_shape=jax.ShapeDtypeStruct((m, n), F32),
        scratch_shapes=[pltpu.VMEM((tm, d), BF16)],
        compiler_params=_cparams("arbitrary", "arbitrary"),
        name="inproj",
    )(x, mod, mod, w)


def _merge_kernel(ya_ref, ybc_ref, ybl_ref, ycc_ref, ycl_ref, glr_ref, wg0_ref, wg1_ref, wg2_ref, wa_ref, wb_ref,
                  wc_ref, o_ref, *, n_ctx_tiles):
    def compute(yb_ref, yc_ref):
        glr = glr_ref[...].astype(BF16)
        g0 = _sigmoid(_dot(glr, wg0_ref[...]))
        g1 = _sigmoid(_dot(glr, wg1_ref[...]))
        g2 = _sigmoid(_dot(glr, wg2_ref[...]))
        merged = (g0 * _dot(ya_ref[...], wa_ref[...]) + g1 * _dot(yb_ref[...], wb_ref[...])
                  + g2 * _dot(yc_ref[...], wc_ref[...]))
        o_ref[...] = merged.astype(BF16)

    is_ctx = pl.program_id(0) < n_ctx_tiles

    @pl.when(is_ctx)
    def _():
        compute(ybc_ref, ycc_ref)

    @pl.when(jnp.logical_not(is_ctx))
    def _():
        compute(ybl_ref, ycl_ref)


def _merge_call(ya, yb_ctx, yb_lat, yc_ctx, yc_lat, proj, w_mg, w_a, w_b, w_c, tm, tn):
    m = ya.shape[0]
    d = w_a.shape[1]
    nj = d // tn
    n_ctx_tiles = yb_ctx.shape[0] // tm
    ctx_map = lambda i, j: (jnp.minimum(i, n_ctx_tiles - 1), 0)
    lat_map = lambda i, j: (jnp.maximum(i - n_ctx_tiles, 0), 0)
    return pl.pallas_call(
        functools.partial(_merge_kernel, n_ctx_tiles=n_ctx_tiles),
        grid=(m // tm, nj),
        in_specs=[
            pl.BlockSpec((tm, A_WIDTH), lambda i, j: (i, 0)),
            pl.BlockSpec((tm, B_VW), ctx_map),
            pl.BlockSpec((tm, B_VW), lat_map),
            pl.BlockSpec((tm, C_QW), ctx_map),
            pl.BlockSpec((tm, C_QW), lat_map),
            pl.BlockSpec((tm, GATE_RANK), lambda i, j: (i, COL_GLR // GATE_RANK)),
            pl.BlockSpec((GATE_RANK, tn), lambda i, j: (0, j)),
            pl.BlockSpec((GATE_RANK, tn), lambda i, j: (0, nj + j)),
            pl.BlockSpec((GATE_RANK, tn), lambda i, j: (0, 2 * nj + j)),
            pl.BlockSpec((A_WIDTH, tn), lambda i, j: (0, j)),
            pl.BlockSpec((B_VW, tn), lambda i, j: (0, j)),
            pl.BlockSpec((C_QW, tn), lambda i, j: (0, j)),
        ],
        out_specs=pl.BlockSpec((tm, tn), lambda i, j: (i, j)),
        out_shape=jax.ShapeDtypeStruct((m, d), BF16),
        compiler_params=_cparams("arbitrary", "arbitrary"),
        name="merge",
    )(ya, yb_ctx, yb_lat, yc_ctx, yc_lat, proj, w_mg, w_mg, w_mg, w_a, w_b, w_c)


def _resid_kernel(a_ref, w_ref, x_ref, gate_ref, o_ref):
    o_ref[...] = x_ref[...] + gate_ref[...] * _dot(a_ref[...], w_ref[...])


def _resid_call(a, w, x, mod, gate_k, group_rows, tm, tn, name, row0=0, n_rows=None):
    m, k = a.shape
    d = w.shape[1]
    n_rows = m if n_rows is None else n_rows
    i0 = row0 // tm
    return pl.pallas_call(
        _resid_kernel,
        grid=(n_rows // tm, d // tn),
        in_specs=[
            pl.BlockSpec((tm, k), lambda i, j: (i0 + i, 0)),
            pl.BlockSpec((k, tn), lambda i, j: (0, j)),
            pl.BlockSpec((tm, tn), lambda i, j: (i0 + i, j)),
            pl.BlockSpec((None, None, 1, tn), lambda i, j: (((i0 + i) * tm) // group_rows, gate_k, 0, j)),
        ],
        out_specs=pl.BlockSpec((tm, tn), lambda i, j: (i, j)),
        out_shape=jax.ShapeDtypeStruct((n_rows, d), F32),
        compiler_params=_cparams("arbitrary", "arbitrary"),
        name=name,
    )(a, w, x, mod)


def _ffn_up_kernel(x_ref, shift_ref, scale_ref, wg_ref, wu_ref, o_ref, h_ref):
    @pl.when(pl.program_id(1) == 0)
    def _():
        _norm_mod_rows(x_ref, shift_ref, scale_ref, h_ref)

    h = h_ref[...]
    o_ref[...] = (_silu(_dot(h, wg_ref[...])) * _dot(h, wu_ref[...])).astype(BF16)


def _ffn_up_call(x, mod, w_gate, w_up, group_rows, tm, tn):
    m, d = x.shape
    n = w_gate.shape[1]
    return pl.pallas_call(
        _ffn_up_kernel,
        grid=(m // tm, n // tn),
        in_specs=[
            pl.BlockSpec((tm, d), lambda i, j: (i, 0)),
            _mod_block(d, 3, tm, group_rows),
            _mod_block(d, 4, tm, group_rows),
            pl.BlockSpec((d, tn), lambda i, j: (0, j)),
            pl.BlockSpec((d, tn), lambda i, j: (0, j)),
        ],
        out_specs=pl.BlockSpec((tm, tn), lambda i, j: (i, j)),
        out_shape=jax.ShapeDtypeStruct((m, n), BF16),
        scratch_shapes=[pltpu.VMEM((tm, d), BF16)],
        compiler_params=_cparams("arbitrary", "arbitrary"),
        name="ffn_up",
    )(x, mod, mod, w_gate, w_up)


def _gmlp_kernel(u_ref, v_ref, gain_ref, ws_ref, bs_ref, o_ref):
    gain = gain_ref[...]
    for c in range(u_ref.shape[0] // A_CHUNK):
        rows = slice(c * A_CHUNK, (c + 1) * A_CHUNK)
        v = _gelu_tanh(v_ref[rows, :])
        ms = jnp.mean(v * v, axis=-1, keepdims=True)
        v = ((v * lax.rsqrt(ms + EPS)) * gain).astype(BF16)
        u = _gelu_tanh(u_ref[rows, :])
        for g in range(A_GROUPS):
            cols = slice(g * A_CH, (g + 1) * A_CH)
            mixed = _dot(ws_ref[g], v[:, cols]) + bs_ref[:, g:g + 1]
            o_ref[rows, cols] = (u[:, cols] * mixed).astype(BF16)


def _gmlp_call(proj, v_gain, w_s, b_s, tm=512):
    m = proj.shape[0]
    return pl.pallas_call(
        _gmlp_kernel,
        grid=(m // tm,),
        in_specs=[
            pl.BlockSpec((tm, A_WIDTH), lambda i: (i, COL_U // A_WIDTH)),
            pl.BlockSpec((tm, A_WIDTH), lambda i: (i, COL_V // A_WIDTH)),
            pl.BlockSpec((1, A_WIDTH), lambda i: (0, 0)),
            pl.BlockSpec((A_GROUPS, A_CHUNK, A_CHUNK), lambda i: (0, 0, 0)),
            pl.BlockSpec((A_CHUNK, A_GROUPS), lambda i: (0, 0)),
        ],
        out_specs=pl.BlockSpec((tm, A_WIDTH), lambda i: (i, 0)),
        out_shape=jax.ShapeDtypeStruct((m, A_WIDTH), BF16),
        compiler_params=_cparams("arbitrary"),
        name="gmlp",
    )(proj, proj, v_gain.reshape(1, A_WIDTH), w_s.astype(BF16), b_s.T)


def _conv_prep_kernel(x_ref, w_ref, o_ref, pad_ref, *, seq, rows_per_step):
    halo = 8
    k_half = B_CONV // 2
    pad_ref[0:halo, :] = jnp.zeros((halo, B_DK), F32)
    pad_ref[halo + seq:2 * halo + seq, :] = jnp.zeros((halo, B_DK), F32)
    pad_ref[halo:halo + seq, :] = x_ref[...]
    kind = pl.program_id(1) // B_HEADS
    q_scale = jnp.where(kind == 0, B_DK ** -0.5, 1.0).astype(F32)
    w = w_ref[...]
    for r0 in range(0, seq, rows_per_step):
        acc = jnp.zeros((rows_per_step, B_DK), F32)
        for j in range(B_CONV):
            start = halo + r0 + j - k_half
            acc = acc + pad_ref[start:start + rows_per_step, :] * w[j:j + 1, :]
        y = _silu(acc)
        ss = jnp.sum(y * y, axis=-1, keepdims=True)
        factor = jnp.where(kind == 2, 1.0, lax.rsqrt(ss + EPS) * q_scale)
        o_ref[r0:r0 + rows_per_step, :] = y * factor


def _conv_prep_call(proj, conv_w, seq, n_seq, row_block0):
    nblk = B_QKV // B_DK
    kern = functools.partial(_conv_prep_kernel, seq=seq, rows_per_step=min(seq, 256))
    return pl.pallas_call(
        kern,
        grid=(n_seq, nblk),
        in_specs=[
            pl.BlockSpec((seq, B_DK), lambda s, c: (row_block0 + s, COL_QKV // B_DK + c)),
            pl.BlockSpec((B_CONV, B_DK), lambda s, c: (0, c)),
        ],
        out_specs=pl.BlockSpec((seq, B_DK), lambda s, c: (s, c)),
        out_shape=jax.ShapeDtypeStruct((n_seq * seq, B_QKV), F32),
        scratch_shapes=[pltpu.VMEM((seq + 16, B_DK), F32)],
        compiler_params=_cparams("arbitrary", "arbitrary"),
        name="conv_prep",
    )(proj, conv_w)


def _split3(x):
    hi = x.astype(BF16)
    r1 = x - hi.astype(F32)
    mid = r1.astype(BF16)
    lo = (r1 - mid.astype(F32)).astype(BF16)
    return hi, mid, lo


def _gate_prep_kernel(ab_ref, alog_ref, dt_ref, o_ref):
    t = ab_ref.shape[0]
    x = ab_ref[:, 0:AB_LANES]
    lane = lax.broadcasted_iota(jnp.int32, (t, AB_LANES), 1)
    z = x + dt_ref[...]
    softplus = jnp.maximum(z, 0.0) + jnp.log(1.0 + jnp.exp(-jnp.abs(z)))
    g = -jnp.exp(alog_ref[...]) * softplus
    beta = _sigmoid(x)
    ri = lax.broadcasted_iota(jnp.int32, (t, t), 0)
    ci = lax.broadcasted_iota(jnp.int32, (t, t), 1)
    same = (ri // B_CHUNK) == (ci // B_CHUNK)
    tril = jnp.where(same & (ci <= ri), 1.0, 0.0).astype(BF16)
    triu = jnp.where(same & (ci >= ri), 1.0, 0.0).astype(BF16)
    hi, mid, lo = _split3(g)
    cum_f = _dot(tril, hi) + _dot(tril, mid) + _dot(tril, lo)
    cum_b = _dot(triu, hi) + _dot(triu, mid) + _dot(triu, lo)
    res = jnp.where(lane < B_HEADS, cum_f, jnp.where(lane < 2 * B_HEADS, cum_b, beta))
    for h in range(B_HEADS):
        o_ref[h] = jnp.concatenate([res[:, h + B_HEADS * c:h + B_HEADS * c + 1] for c in range(4)], axis=1)


def _gate_prep_call(proj, a_log, dt_bias, tm=512):
    m = proj.shape[0]
    lane_pad = jnp.zeros((AB_LANES - 2 * B_HEADS,), F32)
    alog_row = jnp.concatenate([a_log.reshape(-1), lane_pad]).reshape(1, AB_LANES)
    dt_row = jnp.concatenate([dt_bias.reshape(-1), lane_pad]).reshape(1, AB_LANES)
    return pl.pallas_call(
        _gate_prep_kernel,
        grid=(m // tm,),
        in_specs=[
            pl.BlockSpec((tm, AB_PAD), lambda i: (i, COL_AB // AB_PAD)),
            pl.BlockSpec((1, AB_LANES), lambda i: (0, 0)),
            pl.BlockSpec((1, AB_LANES), lambda i: (0, 0)),
        ],
        out_specs=pl.BlockSpec((B_HEADS, tm, 4), lambda i: (0, i, 0)),
        out_shape=jax.ShapeDtypeStruct((B_HEADS, m, 4), F32),
        compiler_params=_cparams("arbitrary"),
        name="gate_prep",
    )(proj, alog_row, dt_row)


def _aligned_rows(block, size):
    if isinstance(block, int):
        return slice(block * size, (block + 1) * size)
    return pl.ds(pl.multiple_of(block * size, size), size)


def _delta_kernel(*refs, seq, zero_init, emit_state):
    it = iter(refs)
    q_ref, k_ref, v_ref, gb_ref, gbr_ref, gate_ref, og_ref = (next(it) for _ in range(7))
    s0_refs = None if zero_init else (next(it), next(it))
    y_ref = next(it)
    s_out = (next(it), next(it)) if emit_state else None
    scr = [tuple(next(it) for _ in range(3)) for _ in range(2)]

    c = B_CHUNK
    r = DELTA_ROWS
    n_chunks = seq // c
    ri = lax.broadcasted_iota(jnp.int32, (r, r), 0)
    ci = lax.broadcasted_iota(jnp.int32, (r, r), 1)
    same = (ri >> B_CHUNK_LOG2) == (ci >> B_CHUNK_LOG2)
    masks = ((same & (ri >= ci), same & (ri > ci)), (same & (ri <= ci), same & (ri < ci)))
    col_chunk = lax.broadcasted_iota(jnp.int32, (B_DK, r), 1) >> B_CHUNK_LOG2

    def prep_group(gi, carry):
        rows = _aligned_rows(gi, r)
        q = q_ref[rows, :]
        k = k_ref[rows, :]
        v = v_ref[rows, :]
        gbv = gb_ref[rows, :]
        gbr = gbr_ref[gi]
        kb = k.astype(BF16)
        kq = _dot_nt(jnp.concatenate([kb, q.astype(BF16)], axis=0), kb)
        kk = kq[0:r, :]
        qk_raw = kq[r:2 * r, :]
        for d in range(2):
            prep_direction(gi, d, rows, q, k, v, gbv, gbr, kk, qk_raw)
        return carry

    def prep_direction(gi, d, rows, q, k, v, gbv, gbr, kk, qk_raw):
        incl, strict = masks[d]
        gc_col = gbv[:, d:d + 1]
        beta_col = gbv[:, 2 + d:3 + d]
        gc = jnp.broadcast_to(gc_col, (r, B_DK))
        beta = jnp.broadcast_to(beta_col, (r, B_DK))
        diff = jnp.broadcast_to(gc_col, (r, r)) - jnp.broadcast_to(gbr[d:d + 1, :], (r, r))
        decay = jnp.where(incl, jnp.exp(jnp.where(incl, diff, 0.0)), 0.0)
        lmat = jnp.where(strict, jnp.broadcast_to(beta_col, (r, r)) * kk * decay, 0.0)
        gam = jnp.exp(gc)
        sol = jnp.concatenate([(beta * gam) * k, beta * v], axis=1)
        pb = (-lmat).astype(BF16)
        sol = sol + _dot(pb, sol.astype(BF16))
        for _ in range(B_CHUNK_LOG2 - 1):
            pb = _dot(pb, pb).astype(BF16)
            sol = sol + _dot(pb, sol.astype(BF16))
        sol = sol.astype(BF16)
        end_row = c - 1 if d == 0 else 0
        g_end = jnp.concatenate(
            [jnp.broadcast_to(gc[j * c + end_row:j * c + end_row + 1, :], (c, B_DK)) for j in range(DELTA_GROUP)],
            axis=0)
        kd = jnp.exp(g_end - gc) * k
        kd_t = kd.T
        lhs = jnp.concatenate([qk_raw * decay] + [jnp.where(col_chunk == j, kd_t, 0.0)
                                                  for j in range(DELTA_GROUP)], axis=0)
        res = _dot(lhs.astype(BF16), sol)
        qo = res[0:r, :]
        ab = res[r:, :]
        q_prime = (gam * q - qo[:, 0:B_DK]).astype(BF16)
        aq_s, b_s, o_s = scr[d]
        for j in range(DELTA_GROUP):
            n = gi * DELTA_GROUP + j
            aq_s[n, 0:B_DK, :] = (-ab[j * B_DK:(j + 1) * B_DK, 0:B_DK]).astype(BF16)
            aq_s[n, B_DK:B_DK + c, :] = q_prime[j * c:(j + 1) * c, :]
            b_s[n] = ab[j * B_DK:(j + 1) * B_DK, B_DK:B_DK + B_DV]
        o_s[rows, :] = qo[:, B_DK:B_DK + B_DV]

    def step(n, s, d):
        aq_s, b_s, o_s = scr[d]
        rows = _aligned_rows(n, c)
        res = _dot(aq_s[n], s.astype(BF16))
        o_s[rows, :] = o_s[rows, :] + res[B_DK:B_DK + c, :]
        end_row = c - 1 if d == 0 else 0
        g_end = gb_ref[pl.ds(n * c + end_row, 1), :][:, d:d + 1]
        return jnp.exp(g_end) * s + res[0:B_DK, :] + b_s[n]

    n_groups = seq // r
    lax.fori_loop(0, n_groups, prep_group, 0, unroll=2 if n_groups % 2 == 0 else 1)

    def scan_body(i, carry):
        s_f, s_b = carry
        return step(i, s_f, 0), step(n_chunks - 1 - i, s_b, 1)

    if zero_init:
        init = (jnp.zeros((B_DK, B_DV), F32), jnp.zeros((B_DK, B_DV), F32))
    else:
        init = (s0_refs[0][...], s0_refs[1][...])
    s_f, s_b = lax.fori_loop(0, n_chunks, scan_body, init, unroll=2)
    if emit_state:
        s_out[0][...] = s_f
        s_out[1][...] = s_b

    og = og_ref[...]
    o_f, o_b = scr[0][2], scr[1][2]
    rstep = min(seq, 256)
    for r0 in range(0, seq, rstep):
        rows = slice(r0, r0 + rstep)
        o = o_f[rows, :] + o_b[rows, :]
        ms = jnp.mean(o * o, axis=-1, keepdims=True)
        y_ref[rows, :] = (((o * lax.rsqrt(ms + EPS)) * og) * _silu(gate_ref[rows, :])).astype(BF16)


def _delta_call(qkv, gb, gbr, proj, o_gain, states, seq, n_seq, row_block0, layer):
    zero_init = states is None
    emit_state = states is None
    kern = functools.partial(_delta_kernel, seq=seq, zero_init=zero_init, emit_state=emit_state)
    n_grp = seq // DELTA_ROWS
    in_specs = [
        pl.BlockSpec((seq, B_DK), lambda s, h: (s, h)),
        pl.BlockSpec((seq, B_DK), lambda s, h: (s, B_HEADS + h)),
        pl.BlockSpec((seq, B_DV), lambda s, h: (s, 2 * B_HEADS + h)),
        pl.BlockSpec((None, seq, 4), lambda s, h: (h, row_block0 + s, 0)),
        pl.BlockSpec((None, n_grp, 4, DELTA_ROWS), lambda s, h: (h, row_block0 + s, 0, 0)),
        pl.BlockSpec((seq, B_DV), lambda s, h: (row_block0 + s, COL_GATE_B // B_DV + h)),
        pl.BlockSpec((1, B_DV), lambda s, h: (0, 0)),
    ]
    args = [qkv, qkv, qkv, gb, gbr, proj, o_gain.reshape(1, B_DV)]
    if not zero_init:
        st_spec = pl.BlockSpec((None, None, None, B_DK, B_DV), lambda s, h: (s, layer, h, 0, 0))
        in_specs += [st_spec, st_spec]
        args += [states[0], states[1]]
    out_specs = [pl.BlockSpec((seq, B_DV), lambda s, h: (s, h))]
    out_shape = [jax.ShapeDtypeStruct((n_seq * seq, B_VW), BF16)]
    if emit_state:
        so_spec = pl.BlockSpec((None, None, B_DK, B_DV), lambda s, h: (s, h, 0, 0))
        out_specs += [so_spec, so_spec]
        out_shape += [jax.ShapeDtypeStruct((n_seq, B_HEADS, B_DK, B_DV), F32)] * 2
    n_chunks = seq // B_CHUNK
    per_dir = [
        pltpu.VMEM((n_chunks, B_DK + B_CHUNK, B_DK), BF16),
        pltpu.VMEM((n_chunks, B_DK, B_DV), F32),
        pltpu.VMEM((seq, B_DV), F32),
    ]
    return pl.pallas_call(
        kern,
        grid=(n_seq, B_HEADS),
        in_specs=in_specs,
        out_specs=out_specs,
        out_shape=out_shape,
        scratch_shapes=per_dir + per_dir,
        compiler_params=_cparams("arbitrary", "arbitrary"),
        name="delta",
    )(*args)


def _rope_tables(n_tokens):
    rows = n_tokens // GRID_W
    row = jnp.repeat(jnp.arange(rows, dtype=F32), GRID_W)
    col = jnp.tile(jnp.arange(GRID_W, dtype=F32), rows)
    axis_dim = HEAD_DIM // 2
    inv = 1.0 / (ROPE_THETA ** (jnp.arange(0, axis_dim, 2, dtype=F32) / axis_dim))
    ang_r = row[:, None] * inv
    ang_c = col[:, None] * inv
    cos = jnp.concatenate([jnp.cos(ang_r), jnp.cos(ang_r), jnp.cos(ang_c), jnp.cos(ang_c)], axis=1)
    sin = jnp.concatenate([-jnp.sin(ang_r), jnp.sin(ang_r), -jnp.sin(ang_c), jnp.sin(ang_c)], axis=1)
    return cos, sin


def _rope(x, cos, sin):
    quarter = HEAD_DIM // 4
    lane = lax.broadcasted_iota(jnp.int32, x.shape, 1)
    first = (lane % (2 * quarter)) < quarter
    partner = jnp.where(first, pltpu.roll(x, HEAD_DIM - quarter, 1), pltpu.roll(x, quarter, 1))
    return x * cos + partner * sin


def _rms_gain(x, gain):
    ms = jnp.mean(x * x, axis=-1, keepdims=True)
    return (x * lax.rsqrt(ms + EPS)) * gain


def _attn_kernel(*refs, seq, n_ctx, rope, tq, emit_k):
    it = iter(refs)
    q_ref, k_ref, v_ref = next(it), next(it), next(it)
    ck_ref, cv_ref = (next(it), next(it)) if n_ctx else (None, None)
    cos_ref, sin_ref = (next(it), next(it)) if rope else (None, None)
    qg_ref, kg_ref = next(it), next(it)
    o_ref = next(it)
    knew_ref = next(it) if emit_k else None
    kt_buf, vbuf, s_even, s_odd, m_even, m_odd = (next(it) for _ in range(6))

    n_keys = seq + n_ctx
    n_tiles = seq // tq
    step = pl.program_id(2)
    qb = jnp.minimum(step, n_tiles - 1)
    kstep = min(seq, 512)

    @pl.when(step == 0)
    def _():
        s_odd[...] = jnp.zeros(s_odd.shape, F32)
        m_odd[...] = jnp.zeros(m_odd.shape, F32)
        kg = kg_ref[...]
        for r0 in range(0, seq, kstep):
            rows = slice(r0, r0 + kstep)
            kn = _rms_gain(k_ref[rows, :], kg)
            if emit_k:
                knew_ref[rows, :] = kn
            if rope:
                kn = _rope(kn, cos_ref[rows, :], sin_ref[rows, :])
            kt_buf[:, rows] = kn.T.astype(BF16)
            vbuf[rows, 0:HEAD_DIM] = v_ref[rows, :].astype(BF16)
        if n_ctx:
            kt_buf[:, seq:n_keys] = ck_ref[...].T.astype(BF16)
            vbuf[seq:n_keys, 0:HEAD_DIM] = cv_ref[...].astype(BF16)
        vbuf[:, HEAD_DIM:2 * HEAD_DIM] = jnp.ones((n_keys, HEAD_DIM), BF16)

    m_rows = C_GROUP * tq
    qg = qg_ref[...]
    q_rows = pl.ds(pl.multiple_of(qb * tq, tq), tq)
    heads = []
    for g in range(C_GROUP):
        qh = _rms_gain(q_ref[:, g * HEAD_DIM:(g + 1) * HEAD_DIM], qg)
        if rope:
            qh = _rope(qh, cos_ref[q_rows, :], sin_ref[q_rows, :])
        heads.append((qh * (HEAD_DIM ** -0.5)).astype(BF16))
    q4 = jnp.concatenate(heads, axis=0)

    blocks = []
    start = 0
    while start < n_keys:
        size = min(KEY_BLOCK, n_keys - start)
        blocks.append((start, size))
        start += size

    def score_and_finish(s_new, m_new, s_old, m_old):
        m_prev = m_old[...]
        m_lane = jnp.full((m_rows, 128), -jnp.inf, F32)
        acc = jnp.zeros((m_rows, 2 * HEAD_DIM), F32)
        for (b0, bs) in blocks:
            s = _dot(q4, kt_buf[:, b0:b0 + bs])
            s_new[:, b0:b0 + bs] = s
            for j in range(bs // 128):
                m_lane = jnp.maximum(m_lane, s[:, j * 128:(j + 1) * 128])
            shift = jnp.concatenate([m_prev] * (bs // 128), axis=1)
            p = jnp.exp(s_old[:, b0:b0 + bs] - shift).astype(BF16)
            acc = acc + _dot(p, vbuf[b0:b0 + bs, :])
        m_new[...] = jnp.broadcast_to(jnp.max(m_lane, axis=-1, keepdims=True), (m_rows, 128))
        o = acc[:, 0:HEAD_DIM] / acc[:, HEAD_DIM:2 * HEAD_DIM]
        for g in range(C_GROUP):
            o_ref[:, g * HEAD_DIM:(g + 1) * HEAD_DIM] = o[g * tq:(g + 1) * tq, :].astype(BF16)

    @pl.when(step % 2 == 0)
    def _():
        score_and_finish(s_even, m_even, s_odd, m_odd)

    @pl.when(step % 2 == 1)
    def _():
        score_and_finish(s_odd, m_odd, s_even, m_even)


def _attn_call(proj, q_gain, k_gain, seq, n_seq, row_block0, tq, cache=None, layer=0, tables=None):
    n_ctx = 0 if cache is None else cache[0].shape[2]
    rope = tables is not None
    emit_k = cache is None
    qw = C_GROUP * HEAD_DIM
    nq = seq // tq
    kern = functools.partial(_attn_kernel, seq=seq, n_ctx=n_ctx, rope=rope, tq=tq, emit_k=emit_k)
    in_specs = [
        pl.BlockSpec((tq, qw),
                     lambda b, h, i: ((row_block0 + b) * nq + jnp.minimum(i, nq - 1), COL_Q // qw + h)),
        pl.BlockSpec((seq, HEAD_DIM), lambda b, h, i: (row_block0 + b, COL_K // HEAD_DIM + h)),
        pl.BlockSpec((seq, HEAD_DIM), lambda b, h, i: (row_block0 + b, COL_VC // HEAD_DIM + h)),
    ]
    args = [proj, proj, proj]
    if n_ctx:
        c_spec = pl.BlockSpec((None, None, n_ctx, HEAD_DIM), lambda b, h, i: (b, layer, 0, h))
        in_specs += [c_spec, c_spec]
        args += [cache[0], cache[1]]
    if rope:
        t_spec = pl.BlockSpec((seq, HEAD_DIM), lambda b, h, i: (0, 0))
        in_specs += [t_spec, t_spec]
        args += [tables[0], tables[1]]
    g_spec = pl.BlockSpec((1, HEAD_DIM), lambda b, h, i: (0, 0))
    in_specs += [g_spec, g_spec]
    args += [q_gain.reshape(1, HEAD_DIM), k_gain.reshape(1, HEAD_DIM)]
    out_specs = [pl.BlockSpec((tq, qw), lambda b, h, i: (b * nq + jnp.maximum(i - 1, 0), h))]
    out_shape = [jax.ShapeDtypeStruct((n_seq * seq, C_QW), BF16)]
    if emit_k:
        out_specs.append(pl.BlockSpec((seq, HEAD_DIM), lambda b, h, i: (b, h)))
        out_shape.append(jax.ShapeDtypeStruct((n_seq * seq, C_KVW), F32))
    n_keys = seq + n_ctx
    return pl.pallas_call(
        kern,
        grid=(n_seq, C_KV, nq + 1),
        in_specs=in_specs,
        out_specs=out_specs,
        out_shape=out_shape,
        scratch_shapes=[
            pltpu.VMEM((HEAD_DIM, n_keys), BF16),
            pltpu.VMEM((n_keys, 2 * HEAD_DIM), BF16),
            pltpu.VMEM((C_GROUP * tq, n_keys), F32),
            pltpu.VMEM((C_GROUP * tq, n_keys), F32),
            pltpu.VMEM((C_GROUP * tq, 128), F32),
            pltpu.VMEM((C_GROUP * tq, 128), F32),
        ],
        compiler_params=_cparams("arbitrary", "arbitrary", "arbitrary"),
        name="attn",
    )(*args)


def _tile(table, layer, m_limit, n_limit):
    tm, tn = table[layer % len(table)]
    return min(tm, m_limit), min(tn, n_limit)


def kernel(x_prompt, x_sample, cache_k, cache_v, state_fwd, state_bwd, c, c_ctx, w_ada, b_ada, w_in, a_v_gain,
           a_w_s, a_b_s, b_conv, b_a_log, b_dt_bias, b_o_gain, c_q_gain, c_k_gain, w_mg, w_br_a, w_br_b, w_br_c,
           w_o, w_gate, w_up, w_down):
    batch, seq, d = x_prompt.shape
    dec_batch, dec_seq, _ = x_sample.shape
    depth = w_in.shape[0]
    n_ctx_rows = batch * seq
    n_lat_rows = dec_batch * dec_seq
    group_rows = dec_seq
    assert n_ctx_rows == group_rows, "context tokens must fill exactly one modulation group"
    n_groups = 1 + dec_batch
    n_tokens = n_ctx_rows + n_lat_rows
    past_len = cache_k.shape[2]

    x = jnp.concatenate([x_prompt.reshape(n_ctx_rows, d), x_sample.reshape(n_lat_rows, d)], axis=0)

    cond_rows = -(-n_groups // 8) * 8
    cond = jnp.concatenate([c_ctx[None, :], c, jnp.zeros((cond_rows - n_groups, d), F32)], axis=0)
    mod_all = _ada_call(cond, w_ada, b_ada)[:, :n_groups].reshape(depth, n_groups, 6, 1, d)

    n_in = w_in.shape[2]
    ab0 = COL_GATE_B + B_VW
    cache_k2 = cache_k.reshape(dec_batch, depth, past_len, C_KVW)
    cache_v2 = cache_v.reshape(dec_batch, depth, past_len, C_KVW)
    tables = _rope_tables(dec_seq)

    new_k, new_v, new_sf, new_sb = [], [], [], []
    y_prompt = y_sample = None
    for l in range(depth):
        mod = mod_all[l]
        w_in_l = jnp.concatenate(
            [w_in[l][:, :ab0], w_in[l][:, ab0 + N_AB:n_in], w_in[l][:, ab0:ab0 + N_AB],
             jnp.zeros((d, AB_PAD - N_AB), F32)], axis=1).astype(BF16)
        proj = _inproj_call(x, mod, w_in_l, group_rows, *_tile(TILES_INPROJ, l, group_rows, N_PROJ))

        y_a = _gmlp_call(proj, a_v_gain[l], a_w_s[l], a_b_s[l])

        gb = _gate_prep_call(proj, b_a_log[l], b_dt_bias[l])
        gbr = gb.reshape(B_HEADS, n_tokens // DELTA_ROWS, DELTA_ROWS, 4).transpose(0, 1, 3, 2)
        qkv_ctx = _conv_prep_call(proj, b_conv[l], seq, batch, 0)
        qkv_lat = _conv_prep_call(proj, b_conv[l], dec_seq, dec_batch, 1)
        yb_ctx, sf, sb = _delta_call(qkv_ctx, gb, gbr, proj, b_o_gain[l], None, seq, batch, 0, l)
        (yb_lat,) = _delta_call(qkv_lat, gb, gbr, proj, b_o_gain[l], (state_fwd, state_bwd), dec_seq, dec_batch,
                                1, l)
        new_sf.append(sf)
        new_sb.append(sb)

        yc_ctx, k_new = _attn_call(proj, c_q_gain[l], c_k_gain[l], seq, batch, 0, tq=seq)
        (yc_lat,) = _attn_call(proj, c_q_gain[l], c_k_gain[l], dec_seq, dec_batch, 1, tq=128,
                               cache=(cache_k2, cache_v2), layer=l, tables=tables)
        new_k.append(k_new.reshape(batch, seq, C_KV, HEAD_DIM))
        new_v.append(proj[:n_ctx_rows, COL_VC:COL_VC + C_KVW].reshape(batch, seq, C_KV, HEAD_DIM))

        merged = _merge_call(y_a, yb_ctx, yb_lat, yc_ctx, yc_lat, proj, w_mg[l].astype(BF16),
                             w_br_a[l].astype(BF16), w_br_b[l].astype(BF16), w_br_c[l].astype(BF16),
                             *_tile(TILES_MERGE, l, group_rows, d))
        x = _resid_call(merged, w_o[l].astype(BF16), x, mod, 2, group_rows, *_tile(TILES_OUT_PROJ, l, group_rows, d),
                        name="out_proj")
        act = _ffn_up_call(x, mod, w_gate[l].astype(BF16), w_up[l].astype(BF16), group_rows,
                           *_tile(TILES_FFN_UP, l, group_rows, w_gate.shape[2]))
        w_down_l = w_down[l].astype(BF16)
        tm, tn = _tile(TILES_FFN_DOWN, l, group_rows, d)
        if l + 1 < depth:
            x = _resid_call(act, w_down_l, x, mod, 5, group_rows, tm, tn, name="ffn_down")
        else:
            y_prompt = _resid_call(act, w_down_l, x, mod, 5, group_rows, tm, tn, name="ffn_down_ctx",
                                   row0=0, n_rows=n_ctx_rows)
            y_sample = _resid_call(act, w_down_l, x, mod, 5, group_rows, tm, tn, name="ffn_down_lat",
                                   row0=n_ctx_rows, n_rows=n_lat_rows)

    return (y_prompt.reshape(batch, seq, d), y_sample.reshape(dec_batch, dec_seq, d),
            jnp.stack(new_k, axis=1), jnp.stack(new_v, axis=1), jnp.stack(new_sf, axis=1), jnp.stack(new_sb, axis=1))
```

```python
import functools

import jax
import jax.numpy as jnp
import numpy as np
from jax import lax
from jax.experimental import pallas as pl
from jax.experimental.pallas import tpu as pltpu

F32 = jnp.float32
BF16 = jnp.bfloat16

EPS = 1e-6
GRID_W = 64
HEAD_DIM = 128
ROPE_THETA = 10000.0
A_GROUPS = 4
A_CH = 128
A_WIDTH = A_GROUPS * A_CH
A_CHUNK = 128
B_HEADS = 4
B_DK = 128
B_DV = 128
B_KW = B_HEADS * B_DK
B_VW = B_HEADS * B_DV
B_QKV = 2 * B_KW + B_VW
B_CONV = 5
B_CHUNK = 64
B_CHUNK_LOG2 = 6
C_HEADS = 8
C_KV = 2
C_GROUP = C_HEADS // C_KV
C_QW = C_HEADS * HEAD_DIM
C_KVW = C_KV * HEAD_DIM
GATE_RANK = 256
N_BRANCH = 3
N_AB = 4 * B_HEADS

COL_U = 0
COL_V = COL_U + A_WIDTH
COL_QKV = COL_V + A_WIDTH
COL_GATE_B = COL_QKV + B_QKV
COL_Q = COL_GATE_B + B_VW
COL_K = COL_Q + C_QW
COL_VC = COL_K + C_KVW
COL_GLR = COL_VC + C_KVW
COL_AB = COL_GLR + GATE_RANK
AB_PAD = 256
AB_LANES = 128
N_PROJ = COL_AB + AB_PAD

MXU_DIM = 256
DELTA_GROUP = MXU_DIM // B_CHUNK
DELTA_ROWS = DELTA_GROUP * B_CHUNK
KEY_BLOCK = 512
VMEM_LIMIT_MB = 56

TILES_INPROJ = ((512, 1280),)
TILES_FFN_UP = ((512, 512),)
TILES_FFN_DOWN = ((1024, 512),)
TILES_OUT_PROJ = ((1024, 1024),)
TILES_MERGE = ((1024, 1024),)


def _cparams(*sem):
    return pltpu.CompilerParams(dimension_semantics=sem, vmem_limit_bytes=VMEM_LIMIT_MB * 1024 * 1024)


def _sigmoid(x):
    return 1.0 / (1.0 + jnp.exp(-x))


def _silu(x):
    return x * _sigmoid(x)


def _gelu_tanh(x):
    return 0.5 * x * (1.0 + jnp.tanh(np.sqrt(2.0 / np.pi).astype(np.float32) * (x + 0.044715 * (x * x * x))))


def _dot(a, b):
    return jnp.dot(a, b, preferred_element_type=F32)


def _dot_nt(a, b):
    return lax.dot_general(a, b, (((1,), (1,)), ((), ())), preferred_element_type=F32)


def _ada_kernel(c_ref, w_ref, b_ref, o_ref):
    c = c_ref[...]
    o_ref[...] = _dot(_silu(c).astype(BF16), w_ref[...].astype(BF16)) + b_ref[...]


def _ada_call(cond, w_ada, b_ada, tn=512):
    depth, d, n = w_ada.shape
    rows = cond.shape[0]
    return pl.pallas_call(
        _ada_kernel,
        grid=(depth, n // tn),
        in_specs=[
            pl.BlockSpec((rows, d), lambda l, j: (0, 0)),
            pl.BlockSpec((None, d, tn), lambda l, j: (l, 0, j)),
            pl.BlockSpec((None, 1, tn), lambda l, j: (l, 0, j)),
        ],
        out_specs=pl.BlockSpec((None, rows, tn), lambda l, j: (l, 0, j)),
        out_shape=jax.ShapeDtypeStruct((depth, rows, n), F32),
        compiler_params=_cparams("arbitrary", "arbitrary"),
        name="ada",
    )(cond, w_ada, b_ada.reshape(depth, 1, n))


def _norm_mod_rows(x_ref, shift_ref, scale_ref, h_ref, chunk=64):
    sc = 1.0 + scale_ref[...]
    sh = shift_ref[...]

    def body(r, carry):
        rows = pl.ds(pl.multiple_of(r * chunk, chunk), chunk)
        x = x_ref[rows, :]
        ms = jnp.mean(x * x, axis=-1, keepdims=True)
        h_ref[rows, :] = ((x * lax.rsqrt(ms + EPS)) * sc + sh).astype(BF16)
        return carry

    lax.fori_loop(0, x_ref.shape[0] // chunk, body, 0)


def _inproj_kernel(x_ref, shift_ref, scale_ref, w_ref, o_ref, h_ref):
    @pl.when(pl.program_id(1) == 0)
    def _():
        _norm_mod_rows(x_ref, shift_ref, scale_ref, h_ref)

    o_ref[...] = _dot(h_ref[...], w_ref[...])


def _mod_block(d, k, tm, group_rows):
    return pl.BlockSpec((None, None, 1, d), lambda i, j: ((i * tm) // group_rows, k, 0, 0))


def _inproj_call(x, mod, w, group_rows, tm, tn):
    m, d = x.shape
    n = w.shape[1]
    return pl.pallas_call(
        _inproj_kernel,
        grid=(m // tm, n // tn),
        in_specs=[
            pl.BlockSpec((tm, d), lambda i, j: (i, 0)),
            _mod_block(d, 0, tm, group_rows),
            _mod_block(d, 1, tm, group_rows),
            pl.BlockSpec((d, tn), lambda i, j: (0, j)),
        ],
        out_specs=pl.BlockSpec((tm, tn), lambda i, j: (i, j)),
        out_shape=jax.ShapeDtypeStruct((m, n), F32),
        scratch_shapes=[pltpu.VMEM((tm, d), BF16)],
        compiler_params=_cparams("arbitrary", "arbitrary"),
        name="inproj",
    )(x, mod, mod, w)


def _merge_kernel(ya_ref, ybc_ref, ybl_ref, ycc_ref, ycl_ref, glr_ref, wg0_ref, wg1_ref, wg2_ref, wa_ref, wb_ref,
                  wc_ref, o_ref, *, n_ctx_tiles):
    def compute(yb_ref, yc_ref):
        glr = glr_ref[...].astype(BF16)
        g0 = _sigmoid(_dot(glr, wg0_ref[...]))
        g1 = _sigmoid(_dot(glr, wg1_ref[...]))
        g2 = _sigmoid(_dot(glr, wg2_ref[...]))
        merged = (g0 * _dot(ya_ref[...], wa_ref[...]) + g1 * _dot(yb_ref[...], wb_ref[...])
                  + g2 * _dot(yc_ref[...], wc_ref[...]))
        o_ref[...] = merged.astype(BF16)

    is_ctx = pl.program_id(0) < n_ctx_tiles

    @pl.when(is_ctx)
    def _():
        compute(ybc_ref, ycc_ref)

    @pl.when(jnp.logical_not(is_ctx))
    def _():
        compute(ybl_ref, ycl_ref)


def _merge_call(ya, yb_ctx, yb_lat, yc_ctx, yc_lat, proj, w_mg, w_a, w_b, w_c, tm, tn):
    m = ya.shape[0]
    d = w_a.shape[1]
    nj = d // tn
    n_ctx_tiles = yb_ctx.shape[0] // tm
    ctx_map = lambda i, j: (jnp.minimum(i, n_ctx_tiles - 1), 0)
    lat_map = lambda i, j: (jnp.maximum(i - n_ctx_tiles, 0), 0)
    return pl.pallas_call(
        functools.partial(_merge_kernel, n_ctx_tiles=n_ctx_tiles),
        grid=(m // tm, nj),
        in_specs=[
            pl.BlockSpec((tm, A_WIDTH), lambda i, j: (i, 0)),
            pl.BlockSpec((tm, B_VW), ctx_map),
            pl.BlockSpec((tm, B_VW), lat_map),
            pl.BlockSpec((tm, C_QW), ctx_map),
            pl.BlockSpec((tm, C_QW), lat_map),
            pl.BlockSpec((tm, GATE_RANK), lambda i, j: (i, COL_GLR // GATE_RANK)),
            pl.BlockSpec((GATE_RANK, tn), lambda i, j: (0, j)),
            pl.BlockSpec((GATE_RANK, tn), lambda i, j: (0, nj + j)),
            pl.BlockSpec((GATE_RANK, tn), lambda i, j: (0, 2 * nj + j)),
            pl.BlockSpec((A_WIDTH, tn), lambda i, j: (0, j)),
            pl.BlockSpec((B_VW, tn), lambda i, j: (0, j)),
            pl.BlockSpec((C_QW, tn), lambda i, j: (0, j)),
        ],
        out_specs=pl.BlockSpec((tm, tn), lambda i, j: (i, j)),
        out_shape=jax.ShapeDtypeStruct((m, d), BF16),
        compiler_params=_cparams("arbitrary", "arbitrary"),
        name="merge",
    )(ya, yb_ctx, yb_lat, yc_ctx, yc_lat, proj, w_mg, w_mg, w_mg, w_a, w_b, w_c)


def _resid_kernel(a_ref, w_ref, x_ref, gate_ref, o_ref):
    o_ref[...] = x_ref[...] + gate_ref[...] * _dot(a_ref[...], w_ref[...])


def _resid_call(a, w, x, mod, gate_k, group_rows, tm, tn, name, row0=0, n_rows=None):
    m, k = a.shape
    d = w.shape[1]
    n_rows = m if n_rows is None else n_rows
    i0 = row0 // tm
    return pl.pallas_call(
        _resid_kernel,
        grid=(n_rows // tm, d // tn),
        in_specs=[
            pl.BlockSpec((tm, k), lambda i, j: (i0 + i, 0)),
            pl.BlockSpec((k, tn), lambda i, j: (0, j)),
            pl.BlockSpec((tm, tn), lambda i, j: (i0 + i, j)),
            pl.BlockSpec((None, None, 1, tn), lambda i, j: (((i0 + i) * tm) // group_rows, gate_k, 0, j)),
        ],
        out_specs=pl.BlockSpec((tm, tn), lambda i, j: (i, j)),
        out_shape=jax.ShapeDtypeStruct((n_rows, d), F32),
        compiler_params=_cparams("arbitrary", "arbitrary"),
        name=name,
    )(a, w, x, mod)


def _ffn_up_kernel(x_ref, shift_ref, scale_ref, wg_ref, wu_ref, o_ref, h_ref):
    @pl.when(pl.program_id(1) == 0)
    def _():
        _norm_mod_rows(x_ref, shift_ref, scale_ref, h_ref)

    h = h_ref[...]
    o_ref[...] = (_silu(_dot(h, wg_ref[...])) * _dot(h, wu_ref[...])).astype(BF16)


def _ffn_up_call(x, mod, w_gate, w_up, group_rows, tm, tn):
    m, d = x.shape
    n = w_gate.shape[1]
    return pl.pallas_call(
        _ffn_up_kernel,
        grid=(m // tm, n // tn),
        in_specs=[
            pl.BlockSpec((tm, d), lambda i, j: (i, 0)),
            _mod_block(d, 3, tm, group_rows),
            _mod_block(d, 4, tm, group_rows),
            pl.BlockSpec((d, tn), lambda i, j: (0, j)),
            pl.BlockSpec((d, tn), lambda i, j: (0, j)),
        ],
        out_specs=pl.BlockSpec((tm, tn), lambda i, j: (i, j)),
        out_shape=jax.ShapeDtypeStruct((m, n), BF16),
        scratch_shapes=[pltpu.VMEM((tm, d), BF16)],
        compiler_params=_cparams("arbitrary", "arbitrary"),
        name="ffn_up",
    )(x, mod, mod, w_gate, w_up)


def _gmlp_kernel(u_ref, v_ref, gain_ref, ws_ref, bs_ref, o_ref):
    gain = gain_ref[...]
    for c in range(u_ref.shape[0] // A_CHUNK):
        rows = slice(c * A_CHUNK, (c + 1) * A_CHUNK)
        v = _gelu_tanh(v_ref[rows, :])
        ms = jnp.mean(v * v, axis=-1, keepdims=True)
        v = ((v * lax.rsqrt(ms + EPS)) * gain).astype(BF16)
        u = _gelu_tanh(u_ref[rows, :])
        for g in range(A_GROUPS):
            cols = slice(g * A_CH, (g + 1) * A_CH)
            mixed = _dot(ws_ref[g], v[:, cols]) + bs_ref[:, g:g + 1]
            o_ref[rows, cols] = (u[:, cols] * mixed).astype(BF16)


def _gmlp_call(proj, v_gain, w_s, b_s, tm=512):
    m = proj.shape[0]
    return pl.pallas_call(
        _gmlp_kernel,
        grid=(m // tm,),
        in_specs=[
            pl.BlockSpec((tm, A_WIDTH), lambda i: (i, COL_U // A_WIDTH)),
            pl.BlockSpec((tm, A_WIDTH), lambda i: (i, COL_V // A_WIDTH)),
            pl.BlockSpec((1, A_WIDTH), lambda i: (0, 0)),
            pl.BlockSpec((A_GROUPS, A_CHUNK, A_CHUNK), lambda i: (0, 0, 0)),
            pl.BlockSpec((A_CHUNK, A_GROUPS), lambda i: (0, 0)),
        ],
        out_specs=pl.BlockSpec((tm, A_WIDTH), lambda i: (i, 0)),
        out_shape=jax.ShapeDtypeStruct((m, A_WIDTH), BF16),
        compiler_params=_cparams("arbitrary"),
        name="gmlp",
    )(proj, proj, v_gain.reshape(1, A_WIDTH), w_s.astype(BF16), b_s.T)


def _conv_prep_kernel(x_ref, w_ref, o_ref, pad_ref, *, seq, rows_per_step):
    halo = 8
    k_half = B_CONV // 2
    pad_ref[0:halo, :] = jnp.zeros((halo, B_DK), F32)
    pad_ref[halo + seq:2 * halo + seq, :] = jnp.zeros((halo, B_DK), F32)
    pad_ref[halo:halo + seq, :] = x_ref[...]
    kind = pl.program_id(1) // B_HEADS
    q_scale = jnp.where(kind == 0, B_DK ** -0.5, 1.0).astype(F32)
    w = w_ref[...]
    for r0 in range(0, seq, rows_per_step):
        acc = jnp.zeros((rows_per_step, B_DK), F32)
        for j in range(B_CONV):
            start = halo + r0 + j - k_half
            acc = acc + pad_ref[start:start + rows_per_step, :] * w[j:j + 1, :]
        y = _silu(acc)
        ss = jnp.sum(y * y, axis=-1, keepdims=True)
        factor = jnp.where(kind == 2, 1.0, lax.rsqrt(ss + EPS) * q_scale)
        o_ref[r0:r0 + rows_per_step, :] = y * factor


def _conv_prep_call(proj, conv_w, seq, n_seq, row_block0):
    nblk = B_QKV // B_DK
    kern = functools.partial(_conv_prep_kernel, seq=seq, rows_per_step=min(seq, 256))
    return pl.pallas_call(
        kern,
        grid=(n_seq, nblk),
        in_specs=[
            pl.BlockSpec((seq, B_DK), lambda s, c: (row_block0 + s, COL_QKV // B_DK + c)),
            pl.BlockSpec((B_CONV, B_DK), lambda s, c: (0, c)),
        ],
        out_specs=pl.BlockSpec((seq, B_DK), lambda s, c: (s, c)),
        out_shape=jax.ShapeDtypeStruct((n_seq * seq, B_QKV), F32),
        scratch_shapes=[pltpu.VMEM((seq + 16, B_DK), F32)],
        compiler_params=_cparams("arbitrary", "arbitrary"),
        name="conv_prep",
    )(proj, conv_w)


def _split3(x):
    hi = x.astype(BF16)
    r1 = x - hi.astype(F32)
    mid = r1.astype(BF16)
    lo = (r1 - mid.astype(F32)).astype(BF16)
    return hi, mid, lo


def _gate_prep_kernel(ab_ref, alog_ref, dt_ref, o_ref):
    t = ab_ref.shape[0]
    x = ab_ref[:, 0:AB_LANES]
    lane = lax.broadcasted_iota(jnp.int32, (t, AB_LANES), 1)
    z = x + dt_ref[...]
    softplus = jnp.maximum(z, 0.0) + jnp.log(1.0 + jnp.exp(-jnp.abs(z)))
    g = -jnp.exp(alog_ref[...]) * softplus
    beta = _sigmoid(x)
    ri = lax.broadcasted_iota(jnp.int32, (t, t), 0)
    ci = lax.broadcasted_iota(jnp.int32, (t, t), 1)
    same = (ri // B_CHUNK) == (ci // B_CHUNK)
    tril = jnp.where(same & (ci <= ri), 1.0, 0.0).astype(BF16)
    triu = jnp.where(same & (ci >= ri), 1.0, 0.0).astype(BF16)
    hi, mid, lo = _split3(g)
    cum_f = _dot(tril, hi) + _dot(tril, mid) + _dot(tril, lo)
    cum_b = _dot(triu, hi) + _dot(triu, mid) + _dot(triu, lo)
    res = jnp.where(lane < B_HEADS, cum_f, jnp.where(lane < 2 * B_HEADS, cum_b, beta))
    for h in range(B_HEADS):
        o_ref[h] = jnp.concatenate([res[:, h + B_HEADS * c:h + B_HEADS * c + 1] for c in range(4)], axis=1)


def _gate_prep_call(proj, a_log, dt_bias, tm=512):
    m = proj.shape[0]
    lane_pad = jnp.zeros((AB_LANES - 2 * B_HEADS,), F32)
    alog_row = jnp.concatenate([a_log.reshape(-1), lane_pad]).reshape(1, AB_LANES)
    dt_row = jnp.concatenate([dt_bias.reshape(-1), lane_pad]).reshape(1, AB_LANES)
    return pl.pallas_call(
        _gate_prep_kernel,
        grid=(m // tm,),
        in_specs=[
            pl.BlockSpec((tm, AB_PAD), lambda i: (i, COL_AB // AB_PAD)),
            pl.BlockSpec((1, AB_LANES), lambda i: (0, 0)),
            pl.BlockSpec((1, AB_LANES), lambda i: (0, 0)),
        ],
        out_specs=pl.BlockSpec((B_HEADS, tm, 4), lambda i: (0, i, 0)),
        out_shape=jax.ShapeDtypeStruct((B_HEADS, m, 4), F32),
        compiler_params=_cparams("arbitrary"),
        name="gate_prep",
    )(proj, alog_row, dt_row)


def _aligned_rows(block, size):
    if isinstance(block, int):
        return slice(block * size, (block + 1) * size)
    return pl.ds(pl.multiple_of(block * size, size), size)


def _delta_kernel(*refs, seq, zero_init, emit_state):
    it = iter(refs)
    q_ref, k_ref, v_ref, gb_ref, gbr_ref, gate_ref, og_ref = (next(it) for _ in range(7))
    s0_refs = None if zero_init else (next(it), next(it))
    y_ref = next(it)
    s_out = (next(it), next(it)) if emit_state else None
    scr = [tuple(next(it) for _ in range(3)) for _ in range(2)]

    c = B_CHUNK
    r = DELTA_ROWS
    n_chunks = seq // c
    ri = lax.broadcasted_iota(jnp.int32, (r, r), 0)
    ci = lax.broadcasted_iota(jnp.int32, (r, r), 1)
    same = (ri >> B_CHUNK_LOG2) == (ci >> B_CHUNK_LOG2)
    masks = ((same & (ri >= ci), same & (ri > ci)), (same & (ri <= ci), same & (ri < ci)))
    col_chunk = lax.broadcasted_iota(jnp.int32, (B_DK, r), 1) >> B_CHUNK_LOG2

    def load_group(gi):
        rows = _aligned_rows(gi, r)
        q = q_ref[rows, :]
        k = k_ref[rows, :]
        kb = k.astype(BF16)
        kq = _dot_nt(jnp.concatenate([kb, q.astype(BF16)], axis=0), kb)
        return dict(gi=gi, rows=rows, q=q, k=k, v=v_ref[rows, :],
                    gbv=gb_ref[rows, :],
                    gbr=gbr_ref[gi],
                    kk=kq[0:r, :], qk_raw=kq[r:2 * r, :])

    def chain_start(g, d):
        incl, strict = masks[d]
        gc_col = g["gbv"][:, d:d + 1]
        beta_col = g["gbv"][:, 2 + d:3 + d]
        gc = jnp.broadcast_to(gc_col, (r, B_DK))
        beta = jnp.broadcast_to(beta_col, (r, B_DK))
        diff = jnp.broadcast_to(gc_col, (r, r)) - jnp.broadcast_to(g["gbr"][d:d + 1, :], (r, r))
        decay = jnp.where(incl, jnp.exp(jnp.where(incl, diff, 0.0)), 0.0)
        lmat = jnp.where(strict, jnp.broadcast_to(beta_col, (r, r)) * g["kk"] * decay, 0.0)
        gam = jnp.exp(gc)
        sol = jnp.concatenate([(beta * gam) * g["k"], beta * g["v"]], axis=1)
        pb = (-lmat).astype(BF16)
        return dict(g=g, d=d, gc=gc, gam=gam, decay=decay, pb=pb, sol=sol + _dot(pb, sol.astype(BF16)))

    def chain_level(ch):
        ch["pb"] = _dot(ch["pb"], ch["pb"]).astype(BF16)
        ch["sol"] = ch["sol"] + _dot(ch["pb"], ch["sol"].astype(BF16))

    def chain_finish(ch):
        g, d, gc, gam, decay = ch["g"], ch["d"], ch["gc"], ch["gam"], ch["decay"]
        gi, rows, q, k, qk_raw = g["gi"], g["rows"], g["q"], g["k"], g["qk_raw"]
        sol = ch["sol"].astype(BF16)
        end_row = c - 1 if d == 0 else 0
        g_end = jnp.concatenate(
            [jnp.broadcast_to(gc[j * c + end_row:j * c + end_row + 1, :], (c, B_DK)) for j in range(DELTA_GROUP)],
            axis=0)
        kd = jnp.exp(g_end - gc) * k
        kd_t = kd.T
        lhs = jnp.concatenate([qk_raw * decay] + [jnp.where(col_chunk == j, kd_t, 0.0)
                                                  for j in range(DELTA_GROUP)], axis=0)
        res = _dot(lhs.astype(BF16), sol)
        qo = res[0:r, :]
        ab = res[r:, :]
        q_prime = (gam * q - qo[:, 0:B_DK]).astype(BF16)
        aq_s, b_s, o_s = scr[d]
        for j in range(DELTA_GROUP):
            n = gi * DELTA_GROUP + j
            aq_s[n, 0:B_DK, :] = (-ab[j * B_DK:(j + 1) * B_DK, 0:B_DK]).astype(BF16)
            aq_s[n, B_DK:B_DK + c, :] = q_prime[j * c:(j + 1) * c, :]
            b_s[n] = ab[j * B_DK:(j + 1) * B_DK, B_DK:B_DK + B_DV]
        o_s[rows, :] = qo[:, B_DK:B_DK + B_DV]

    def step(n, s, d):
        aq_s, b_s, o_s = scr[d]
        rows = _aligned_rows(n, c)
        res = _dot(aq_s[n], s.astype(BF16))
        o_s[rows, :] = o_s[rows, :] + res[B_DK:B_DK + c, :]
        end_row = c - 1 if d == 0 else 0
        g_end = gb_ref[pl.ds(n * c + end_row, 1), :][:, d:d + 1]
        return jnp.exp(g_end) * s + res[0:B_DK, :] + b_s[n]

    n_groups = seq // r
    groups_per_iter = 2 if n_groups % 2 == 0 else 1

    def prep_body(i, carry):
        chains = [chain_start(g, d)
                  for g in [load_group(i * groups_per_iter + j) for j in range(groups_per_iter)]
                  for d in range(2)]
        for _ in range(B_CHUNK_LOG2 - 1):
            for ch in chains:
                chain_level(ch)
        for ch in chains:
            chain_finish(ch)
        return carry

    lax.fori_loop(0, n_groups // groups_per_iter, prep_body, 0)

    def scan_body(i, carry):
        s_f, s_b = carry
        return step(i, s_f, 0), step(n_chunks - 1 - i, s_b, 1)

    if zero_init:
        init = (jnp.zeros((B_DK, B_DV), F32), jnp.zeros((B_DK, B_DV), F32))
    else:
        init = (s0_refs[0][...], s0_refs[1][...])
    s_f, s_b = lax.fori_loop(0, n_chunks, scan_body, init, unroll=2)
    if emit_state:
        s_out[0][...] = s_f
        s_out[1][...] = s_b

    og = og_ref[...]
    o_f, o_b = scr[0][2], scr[1][2]
    rstep = min(seq, 256)
    for r0 in range(0, seq, rstep):
        rows = slice(r0, r0 + rstep)
        o = o_f[rows, :] + o_b[rows, :]
        ms = jnp.mean(o * o, axis=-1, keepdims=True)
        y_ref[rows, :] = (((o * lax.rsqrt(ms + EPS)) * og) * _silu(gate_ref[rows, :])).astype(BF16)


def _delta_call(qkv, gb, gbr, proj, o_gain, states, seq, n_seq, row_block0, layer):
    zero_init = states is None
    emit_state = states is None
    kern = functools.partial(_delta_kernel, seq=seq, zero_init=zero_init, emit_state=emit_state)
    n_grp = seq // DELTA_ROWS
    in_specs = [
        pl.BlockSpec((seq, B_DK), lambda s, h: (s, h)),
        pl.BlockSpec((seq, B_DK), lambda s, h: (s, B_HEADS + h)),
        pl.BlockSpec((seq, B_DV), lambda s, h: (s, 2 * B_HEADS + h)),
        pl.BlockSpec((None, seq, 4), lambda s, h: (h, row_block0 + s, 0)),
        pl.BlockSpec((None, n_grp, 4, DELTA_ROWS), lambda s, h: (h, row_block0 + s, 0, 0)),
        pl.BlockSpec((seq, B_DV), lambda s, h: (row_block0 + s, COL_GATE_B // B_DV + h)),
        pl.BlockSpec((1, B_DV), lambda s, h: (0, 0)),
    ]
    args = [qkv, qkv, qkv, gb, gbr, proj, o_gain.reshape(1, B_DV)]
    if not zero_init:
        st_spec = pl.BlockSpec((None, None, None, B_DK, B_DV), lambda s, h: (s, layer, h, 0, 0))
        in_specs += [st_spec, st_spec]
        args += [states[0], states[1]]
    out_specs = [pl.BlockSpec((seq, B_DV), lambda s, h: (s, h))]
    out_shape = [jax.ShapeDtypeStruct((n_seq * seq, B_VW), BF16)]
    if emit_state:
        so_spec = pl.BlockSpec((None, None, B_DK, B_DV), lambda s, h: (s, h, 0, 0))
        out_specs += [so_spec, so_spec]
        out_shape += [jax.ShapeDtypeStruct((n_seq, B_HEADS, B_DK, B_DV), F32)] * 2
    n_chunks = seq // B_CHUNK
    per_dir = [
        pltpu.VMEM((n_chunks, B_DK + B_CHUNK, B_DK), BF16),
        pltpu.VMEM((n_chunks, B_DK, B_DV), F32),
        pltpu.VMEM((seq, B_DV), F32),
    ]
    return pl.pallas_call(
        kern,
        grid=(n_seq, B_HEADS),
        in_specs=in_specs,
        out_specs=out_specs,
        out_shape=out_shape,
        scratch_shapes=per_dir + per_dir,
        compiler_params=_cparams("arbitrary", "arbitrary"),
        name="delta",
    )(*args)


def _rope_tables(n_tokens):
    rows = n_tokens // GRID_W
    row = jnp.repeat(jnp.arange(rows, dtype=F32), GRID_W)
    col = jnp.tile(jnp.arange(GRID_W, dtype=F32), rows)
    axis_dim = HEAD_DIM // 2
    inv = 1.0 / (ROPE_THETA ** (jnp.arange(0, axis_dim, 2, dtype=F32) / axis_dim))
    ang_r = row[:, None] * inv
    ang_c = col[:, None] * inv
    cos = jnp.concatenate([jnp.cos(ang_r), jnp.cos(ang_r), jnp.cos(ang_c), jnp.cos(ang_c)], axis=1)
    sin = jnp.concatenate([-jnp.sin(ang_r), jnp.sin(ang_r), -jnp.sin(ang_c), jnp.sin(ang_c)], axis=1)
    return cos, sin


def _rope(x, cos, sin):
    quarter = HEAD_DIM // 4
    lane = lax.broadcasted_iota(jnp.int32, x.shape, 1)
    first = (lane % (2 * quarter)) < quarter
    partner = jnp.where(first, pltpu.roll(x, HEAD_DIM - quarter, 1), pltpu.roll(x, quarter, 1))
    return x * cos + partner * sin


def _rms_gain(x, gain):
    ms = jnp.mean(x * x, axis=-1, keepdims=True)
    return (x * lax.rsqrt(ms + EPS)) * gain


def _q_prep_kernel(q_ref, cos_ref, sin_ref, qg_ref, o_ref, *, n_ctx_tiles):
    qg = qg_ref[...]

    def run(use_rope):
        for h in range(C_HEADS):
            cols = slice(h * HEAD_DIM, (h + 1) * HEAD_DIM)
            qh = _rms_gain(q_ref[:, cols], qg)
            if use_rope:
                qh = _rope(qh, cos_ref[...], sin_ref[...])
            o_ref[:, cols] = (qh * (HEAD_DIM ** -0.5)).astype(BF16)

    is_ctx = pl.program_id(0) < n_ctx_tiles

    @pl.when(is_ctx)
    def _():
        run(False)

    @pl.when(jnp.logical_not(is_ctx))
    def _():
        run(True)


def _q_prep_call(proj, q_gain, tables, n_ctx_rows, seq_rows, tm=512):
    m = proj.shape[0]
    tiles_per_seq = seq_rows // tm
    t_spec = pl.BlockSpec((tm, HEAD_DIM), lambda i: (i % tiles_per_seq, 0))
    return pl.pallas_call(
        functools.partial(_q_prep_kernel, n_ctx_tiles=n_ctx_rows // tm),
        grid=(m // tm,),
        in_specs=[
            pl.BlockSpec((tm, C_QW), lambda i: (i, COL_Q // C_QW)),
            t_spec,
            t_spec,
            pl.BlockSpec((1, HEAD_DIM), lambda i: (0, 0)),
        ],
        out_specs=pl.BlockSpec((tm, C_QW), lambda i: (i, 0)),
        out_shape=jax.ShapeDtypeStruct((m, C_QW), BF16),
        compiler_params=_cparams("arbitrary"),
        name="q_prep",
    )(proj, tables[0], tables[1], q_gain.reshape(1, HEAD_DIM))


def _attn_kernel(*refs, seq, n_ctx, rope, tq, emit_k):
    it = iter(refs)
    q_ref, k_ref, v_ref = next(it), next(it), next(it)
    ck_ref, cv_ref = (next(it), next(it)) if n_ctx else (None, None)
    cos_ref, sin_ref = (next(it), next(it)) if rope else (None, None)
    kg_ref = next(it)
    o_ref = next(it)
    knew_ref = next(it) if emit_k else None
    kt_buf, vbuf, s_even, s_odd, m_even, m_odd = (next(it) for _ in range(6))

    n_keys = seq + n_ctx
    step = pl.program_id(2)
    kstep = min(seq, 512)

    @pl.when(step == 0)
    def _():
        s_odd[...] = jnp.zeros(s_odd.shape, F32)
        m_odd[...] = jnp.zeros(m_odd.shape, F32)
        kg = kg_ref[...]
        for r0 in range(0, seq, kstep):
            rows = slice(r0, r0 + kstep)
            kn = _rms_gain(k_ref[rows, :], kg)
            if emit_k:
                knew_ref[rows, :] = kn
            if rope:
                kn = _rope(kn, cos_ref[rows, :], sin_ref[rows, :])
            kt_buf[:, rows] = kn.T.astype(BF16)
            vbuf[rows, 0:HEAD_DIM] = v_ref[rows, :].astype(BF16)
        if n_ctx:
            kt_buf[:, seq:n_keys] = ck_ref[...].T.astype(BF16)
            vbuf[seq:n_keys, 0:HEAD_DIM] = cv_ref[...].astype(BF16)
        vbuf[:, HEAD_DIM:2 * HEAD_DIM] = jnp.ones((n_keys, HEAD_DIM), BF16)

    m_rows = C_GROUP * tq
    q4 = jnp.concatenate([q_ref[:, g * HEAD_DIM:(g + 1) * HEAD_DIM] for g in range(C_GROUP)], axis=0)

    blocks = []
    start = 0
    while start < n_keys:
        size = min(KEY_BLOCK, n_keys - start)
        blocks.append((start, size))
        start += size

    def score_and_finish(s_new, m_new, s_old, m_old):
        m_prev = m_old[...]
        m_lane = jnp.full((m_rows, 128), -jnp.inf, F32)
        acc = jnp.zeros((m_rows, 2 * HEAD_DIM), F32)
        for (b0, bs) in blocks:
            s = _dot(q4, kt_buf[:, b0:b0 + bs])
            s_new[:, b0:b0 + bs] = s
            for j in range(bs // 128):
                m_lane = jnp.maximum(m_lane, s[:, j * 128:(j + 1) * 128])
            shift = jnp.concatenate([m_prev] * (bs // 128), axis=1)
            p = jnp.exp(s_old[:, b0:b0 + bs] - shift).astype(BF16)
            acc = acc + _dot(p, vbuf[b0:b0 + bs, :])
        m_new[...] = jnp.broadcast_to(jnp.max(m_lane, axis=-1, keepdims=True), (m_rows, 128))
        o = acc[:, 0:HEAD_DIM] / acc[:, HEAD_DIM:2 * HEAD_DIM]
        for g in range(C_GROUP):
            o_ref[:, g * HEAD_DIM:(g + 1) * HEAD_DIM] = o[g * tq:(g + 1) * tq, :].astype(BF16)

    @pl.when(step % 2 == 0)
    def _():
        score_and_finish(s_even, m_even, s_odd, m_odd)

    @pl.when(step % 2 == 1)
    def _():
        score_and_finish(s_odd, m_odd, s_even, m_even)


def _attn_call(proj, q_prepared, k_gain, seq, n_seq, row_block0, tq, cache=None, layer=0, tables=None):
    n_ctx = 0 if cache is None else cache[0].shape[2]
    rope = tables is not None
    emit_k = cache is None
    qw = C_GROUP * HEAD_DIM
    nq = seq // tq
    kern = functools.partial(_attn_kernel, seq=seq, n_ctx=n_ctx, rope=rope, tq=tq, emit_k=emit_k)
    in_specs = [
        pl.BlockSpec((tq, qw), lambda b, h, i: ((row_block0 + b) * nq + jnp.minimum(i, nq - 1), h)),
        pl.BlockSpec((seq, HEAD_DIM), lambda b, h, i: (row_block0 + b, COL_K // HEAD_DIM + h)),
        pl.BlockSpec((seq, HEAD_DIM), lambda b, h, i: (row_block0 + b, COL_VC // HEAD_DIM + h)),
    ]
    args = [q_prepared, proj, proj]
    if n_ctx:
        c_spec = pl.BlockSpec((None, None, n_ctx, HEAD_DIM), lambda b, h, i: (b, layer, 0, h))
        in_specs += [c_spec, c_spec]
        args += [cache[0], cache[1]]
    if rope:
        t_spec = pl.BlockSpec((seq, HEAD_DIM), lambda b, h, i: (0, 0))
        in_specs += [t_spec, t_spec]
        args += [tables[0], tables[1]]
    in_specs.append(pl.BlockSpec((1, HEAD_DIM), lambda b, h, i: (0, 0)))
    args.append(k_gain.reshape(1, HEAD_DIM))
    out_specs = [pl.BlockSpec((tq, qw), lambda b, h, i: (b * nq + jnp.maximum(i - 1, 0), h))]
    out_shape = [jax.ShapeDtypeStruct((n_seq * seq, C_QW), BF16)]
    if emit_k:
        out_specs.append(pl.BlockSpec((seq, HEAD_DIM), lambda b, h, i: (b, h)))
        out_shape.append(jax.ShapeDtypeStruct((n_seq * seq, C_KVW), F32))
    n_keys = seq + n_ctx
    return pl.pallas_call(
        kern,
        grid=(n_seq, C_KV, nq + 1),
        in_specs=in_specs,
        out_specs=out_specs,
        out_shape=out_shape,
        scratch_shapes=[
            pltpu.VMEM((HEAD_DIM, n_keys), BF16),
            pltpu.VMEM((n_keys, 2 * HEAD_DIM), BF16),
            pltpu.VMEM((C_GROUP * tq, n_keys), F32),
            pltpu.VMEM((C_GROUP * tq, n_keys), F32),
            pltpu.VMEM((C_GROUP * tq, 128), F32),
            pltpu.VMEM((C_GROUP * tq, 128), F32),
        ],
        compiler_params=_cparams("arbitrary", "arbitrary", "arbitrary"),
        name="attn",
    )(*args)


def _tile(table, layer, m_limit, n_limit):
    tm, tn = table[layer % len(table)]
    return min(tm, m_limit), min(tn, n_limit)


def kernel(x_prompt, x_sample, cache_k, cache_v, state_fwd, state_bwd, c, c_ctx, w_ada, b_ada, w_in, a_v_gain,
           a_w_s, a_b_s, b_conv, b_a_log, b_dt_bias, b_o_gain, c_q_gain, c_k_gain, w_mg, w_br_a, w_br_b, w_br_c,
           w_o, w_gate, w_up, w_down):
    batch, seq, d = x_prompt.shape
    dec_batch, dec_seq, _ = x_sample.shape
    depth = w_in.shape[0]
    n_ctx_rows = batch * seq
    n_lat_rows = dec_batch * dec_seq
    group_rows = dec_seq
    assert n_ctx_rows == group_rows, "context tokens must fill exactly one modulation group"
    n_groups = 1 + dec_batch
    n_tokens = n_ctx_rows + n_lat_rows
    past_len = cache_k.shape[2]

    x = jnp.concatenate([x_prompt.reshape(n_ctx_rows, d), x_sample.reshape(n_lat_rows, d)], axis=0)

    cond_rows = -(-n_groups // 8) * 8
    cond = jnp.concatenate([c_ctx[None, :], c, jnp.zeros((cond_rows - n_groups, d), F32)], axis=0)
    mod_all = _ada_call(cond, w_ada, b_ada)[:, :n_groups].reshape(depth, n_groups, 6, 1, d)

    n_in = w_in.shape[2]
    ab0 = COL_GATE_B + B_VW
    cache_k2 = cache_k.reshape(dec_batch, depth, past_len, C_KVW)
    cache_v2 = cache_v.reshape(dec_batch, depth, past_len, C_KVW)
    tables = _rope_tables(dec_seq)

    new_k, new_v, new_sf, new_sb = [], [], [], []
    y_prompt = y_sample = None
    for l in range(depth):
        mod = mod_all[l]
        w_in_l = jnp.concatenate(
            [w_in[l][:, :ab0], w_in[l][:, ab0 + N_AB:n_in], w_in[l][:, ab0:ab0 + N_AB],
             jnp.zeros((d, AB_PAD - N_AB), F32)], axis=1).astype(BF16)
        proj = _inproj_call(x, mod, w_in_l, group_rows, *_tile(TILES_INPROJ, l, group_rows, N_PROJ))

        y_a = _gmlp_call(proj, a_v_gain[l], a_w_s[l], a_b_s[l])

        gb = _gate_prep_call(proj, b_a_log[l], b_dt_bias[l])
        gbr = gb.reshape(B_HEADS, n_tokens // DELTA_ROWS, DELTA_ROWS, 4).transpose(0, 1, 3, 2)
        qkv_ctx = _conv_prep_call(proj, b_conv[l], seq, batch, 0)
        qkv_lat = _conv_prep_call(proj, b_conv[l], dec_seq, dec_batch, 1)
        yb_ctx, sf, sb = _delta_call(qkv_ctx, gb, gbr, proj, b_o_gain[l], None, seq, batch, 0, l)
        (yb_lat,) = _delta_call(qkv_lat, gb, gbr, proj, b_o_gain[l], (state_fwd, state_bwd), dec_seq, dec_batch,
                                1, l)
        new_sf.append(sf)
        new_sb.append(sb)

        q_prepared = _q_prep_call(proj, c_q_gain[l], tables, n_ctx_rows, dec_seq)
        yc_ctx, k_new = _attn_call(proj, q_prepared, c_k_gain[l], seq, batch, 0, tq=seq)
        (yc_lat,) = _attn_call(proj, q_prepared, c_k_gain[l], dec_seq, dec_batch, 1, tq=128,
                               cache=(cache_k2, cache_v2), layer=l, tables=tables)
        new_k.append(k_new.reshape(batch, seq, C_KV, HEAD_DIM))
        new_v.append(proj[:n_ctx_rows, COL_VC:COL_VC + C_KVW].reshape(batch, seq, C_KV, HEAD_DIM))

        merged = _merge_call(y_a, yb_ctx, yb_lat, yc_ctx, yc_lat, proj, w_mg[l].astype(BF16),
                             w_br_a[l].astype(BF16), w_br_b[l].astype(BF16), w_br_c[l].astype(BF16),
                             *_tile(TILES_MERGE, l, group_rows, d))
        x = _resid_call(merged, w_o[l].astype(BF16), x, mod, 2, group_rows, *_tile(TILES_OUT_PROJ, l, group_rows, d),
                        name="out_proj")
        act = _ffn_up_call(x, mod, w_gate[l].astype(BF16), w_up[l].astype(BF16), group_rows,
                           *_tile(TILES_FFN_UP, l, group_rows, w_gate.shape[2]))
        w_down_l = w_down[l].astype(BF16)
        tm, tn = _tile(TILES_FFN_DOWN, l, group_rows, d)
        if l + 1 < depth:
            x = _resid_call(act, w_down_l, x, mod, 5, group_rows, tm, tn, name="ffn_down")
        else:
            y_prompt = _resid_call(act, w_down_l, x, mod, 5, group_rows, tm, tn, name="ffn_down_ctx",
                                   row0=0, n_rows=n_ctx_rows)
            y_sample = _resid_call(act, w_down_l, x, mod, 5, group_rows, tm, tn, name="ffn_down_lat",
                                   row0=n_ctx_rows, n_rows=n_lat_rows)

    return (y_prompt.reshape(batch, seq, d), y_sample.reshape(dec_batch, dec_seq, d),
            jnp.stack(new_k, axis=1), jnp.stack(new_v, axis=1), jnp.stack(new_sf, axis=1), jnp.stack(new_sb, axis=1))
```

```python
import functools

import jax
import jax.numpy as jnp
import numpy as np
from jax import lax
from jax.experimental import pallas as pl
from jax.experimental.pallas import tpu as pltpu

F32 = jnp.float32
BF16 = jnp.bfloat16

EPS = 1e-6
GRID_W = 64
HEAD_DIM = 128
ROPE_THETA = 10000.0
A_GROUPS = 4
A_CH = 128
A_WIDTH = A_GROUPS * A_CH
A_CHUNK = 128
B_HEADS = 4
B_DK = 128
B_DV = 128
B_KW = B_HEADS * B_DK
B_VW = B_HEADS * B_DV
B_QKV = 2 * B_KW + B_VW
B_CONV = 5
B_CHUNK = 64
B_CHUNK_LOG2 = 6
C_HEADS = 8
C_KV = 2
C_GROUP = C_HEADS // C_KV
C_QW = C_HEADS * HEAD_DIM
C_KVW = C_KV * HEAD_DIM
GATE_RANK = 256
N_BRANCH = 3
N_AB = 4 * B_HEADS

COL_U = 0
COL_V = COL_U + A_WIDTH
COL_QKV = COL_V + A_WIDTH
COL_GATE_B = COL_QKV + B_QKV
COL_Q = COL_GATE_B + B_VW
COL_K = COL_Q + C_QW
COL_VC = COL_K + C_KVW
COL_GLR = COL_VC + C_KVW
COL_AB = COL_GLR + GATE_RANK
AB_PAD = 256
AB_LANES = 128
N_PROJ = COL_AB + AB_PAD

MXU_DIM = 256
DELTA_GROUP = MXU_DIM // B_CHUNK
DELTA_ROWS = DELTA_GROUP * B_CHUNK
KEY_BLOCK = 512
VMEM_LIMIT_MB = 56

TILES_INPROJ = ((1024, 1280),)
TILES_FFN_UP = ((1024, 512),)
TILES_FFN_DOWN = ((1024, 512),)
TILES_OUT_PROJ = ((1024, 1024),)
TILES_MERGE = ((1024, 1024),)


def _cparams(*sem):
    return pltpu.CompilerParams(dimension_semantics=sem, vmem_limit_bytes=VMEM_LIMIT_MB * 1024 * 1024)


def _sigmoid(x):
    return 1.0 / (1.0 + jnp.exp(-x))


def _silu(x):
    return x * _sigmoid(x)


def _gelu_tanh(x):
    return 0.5 * x * (1.0 + jnp.tanh(np.sqrt(2.0 / np.pi).astype(np.float32) * (x + 0.044715 * (x * x * x))))


def _dot(a, b):
    return jnp.dot(a, b, preferred_element_type=F32)


def _dot_nt(a, b):
    return lax.dot_general(a, b, (((1,), (1,)), ((), ())), preferred_element_type=F32)


def _ada_kernel(c_ref, w_ref, b_ref, o_ref):
    c = c_ref[...]
    o_ref[...] = _dot(_silu(c).astype(BF16), w_ref[...].astype(BF16)) + b_ref[...]


def _ada_call(cond, w_ada, b_ada, tn=512):
    depth, d, n = w_ada.shape
    rows = cond.shape[0]
    return pl.pallas_call(
        _ada_kernel,
        grid=(depth, n // tn),
        in_specs=[
            pl.BlockSpec((rows, d), lambda l, j: (0, 0)),
            pl.BlockSpec((None, d, tn), lambda l, j: (l, 0, j)),
            pl.BlockSpec((None, 1, tn), lambda l, j: (l, 0, j)),
        ],
        out_specs=pl.BlockSpec((None, rows, tn), lambda l, j: (l, 0, j)),
        out_shape=jax.ShapeDtypeStruct((depth, rows, n), F32),
        compiler_params=_cparams("arbitrary", "arbitrary"),
        name="ada",
    )(cond, w_ada, b_ada.reshape(depth, 1, n))


def _norm_mod_rows(x_ref, shift_ref, scale_ref, h_ref, chunk=64):
    sc = 1.0 + scale_ref[...]
    sh = shift_ref[...]

    def body(r, carry):
        rows = pl.ds(pl.multiple_of(r * chunk, chunk), chunk)
        x = x_ref[rows, :]
        ms = jnp.mean(x * x, axis=-1, keepdims=True)
        h_ref[rows, :] = ((x * lax.rsqrt(ms + EPS)) * sc + sh).astype(BF16)
        return carry

    lax.fori_loop(0, x_ref.shape[0] // chunk, body, 0)


def _mod_block(d, k, tm, group_rows):
    return pl.BlockSpec((None, None, 1, d), lambda i, j: ((i * tm) // group_rows, k, 0, 0))


def _norm_kernel(x_ref, shift_ref, scale_ref, h_ref):
    _norm_mod_rows(x_ref, shift_ref, scale_ref, h_ref)


def _norm_call(x, mod, shift_k, scale_k, group_rows, tm=256):
    m, d = x.shape
    tm = min(tm, group_rows)
    return pl.pallas_call(
        _norm_kernel,
        grid=(m // tm, 1),
        in_specs=[
            pl.BlockSpec((tm, d), lambda i, j: (i, 0)),
            _mod_block(d, shift_k, tm, group_rows),
            _mod_block(d, scale_k, tm, group_rows),
        ],
        out_specs=pl.BlockSpec((tm, d), lambda i, j: (i, 0)),
        out_shape=jax.ShapeDtypeStruct((m, d), BF16),
        compiler_params=_cparams("arbitrary", "arbitrary"),
        name="norm_mod",
    )(x, mod, mod)


def _inproj_kernel(h_ref, w_ref, o_ref):
    o_ref[...] = _dot(h_ref[...], w_ref[...])


def _inproj_call(h, w, tm, tn):
    m, d = h.shape
    n = w.shape[1]
    return pl.pallas_call(
        _inproj_kernel,
        grid=(m // tm, n // tn),
        in_specs=[
            pl.BlockSpec((tm, d), lambda i, j: (i, 0)),
            pl.BlockSpec((d, tn), lambda i, j: (0, j)),
        ],
        out_specs=pl.BlockSpec((tm, tn), lambda i, j: (i, j)),
        out_shape=jax.ShapeDtypeStruct((m, n), F32),
        compiler_params=_cparams("arbitrary", "arbitrary"),
        name="inproj",
    )(h, w)


def _merge_kernel(ya_ref, ybc_ref, ybl_ref, ycc_ref, ycl_ref, glr_ref, wg0_ref, wg1_ref, wg2_ref, wa_ref, wb_ref,
                  wc_ref, o_ref, *, n_ctx_tiles):
    def compute(yb_ref, yc_ref):
        glr = glr_ref[...].astype(BF16)
        g0 = _sigmoid(_dot(glr, wg0_ref[...]))
        g1 = _sigmoid(_dot(glr, wg1_ref[...]))
        g2 = _sigmoid(_dot(glr, wg2_ref[...]))
        merged = (g0 * _dot(ya_ref[...], wa_ref[...]) + g1 * _dot(yb_ref[...], wb_ref[...])
                  + g2 * _dot(yc_ref[...], wc_ref[...]))
        o_ref[...] = merged.astype(BF16)

    is_ctx = pl.program_id(0) < n_ctx_tiles

    @pl.when(is_ctx)
    def _():
        compute(ybc_ref, ycc_ref)

    @pl.when(jnp.logical_not(is_ctx))
    def _():
        compute(ybl_ref, ycl_ref)


def _merge_call(ya, yb_ctx, yb_lat, yc_ctx, yc_lat, proj, w_mg, w_a, w_b, w_c, tm, tn):
    m = ya.shape[0]
    d = w_a.shape[1]
    nj = d // tn
    n_ctx_tiles = yb_ctx.shape[0] // tm
    ctx_map = lambda i, j: (jnp.minimum(i, n_ctx_tiles - 1), 0)
    lat_map = lambda i, j: (jnp.maximum(i - n_ctx_tiles, 0), 0)
    return pl.pallas_call(
        functools.partial(_merge_kernel, n_ctx_tiles=n_ctx_tiles),
        grid=(m // tm, nj),
        in_specs=[
            pl.BlockSpec((tm, A_WIDTH), lambda i, j: (i, 0)),
            pl.BlockSpec((tm, B_VW), ctx_map),
            pl.BlockSpec((tm, B_VW), lat_map),
            pl.BlockSpec((tm, C_QW), ctx_map),
            pl.BlockSpec((tm, C_QW), lat_map),
            pl.BlockSpec((tm, GATE_RANK), lambda i, j: (i, COL_GLR // GATE_RANK)),
            pl.BlockSpec((GATE_RANK, tn), lambda i, j: (0, j)),
            pl.BlockSpec((GATE_RANK, tn), lambda i, j: (0, nj + j)),
            pl.BlockSpec((GATE_RANK, tn), lambda i, j: (0, 2 * nj + j)),
            pl.BlockSpec((A_WIDTH, tn), lambda i, j: (0, j)),
            pl.BlockSpec((B_VW, tn), lambda i, j: (0, j)),
            pl.BlockSpec((C_QW, tn), lambda i, j: (0, j)),
        ],
        out_specs=pl.BlockSpec((tm, tn), lambda i, j: (i, j)),
        out_shape=jax.ShapeDtypeStruct((m, d), BF16),
        compiler_params=_cparams("arbitrary", "arbitrary"),
        name="merge",
    )(ya, yb_ctx, yb_lat, yc_ctx, yc_lat, proj, w_mg, w_mg, w_mg, w_a, w_b, w_c)


def _resid_kernel(a_ref, w_ref, x_ref, gate_ref, o_ref):
    o_ref[...] = x_ref[...] + gate_ref[...] * _dot(a_ref[...], w_ref[...])


def _resid_call(a, w, x, mod, gate_k, group_rows, tm, tn, name, row0=0, n_rows=None):
    m, k = a.shape
    d = w.shape[1]
    n_rows = m if n_rows is None else n_rows
    i0 = row0 // tm
    return pl.pallas_call(
        _resid_kernel,
        grid=(n_rows // tm, d // tn),
        in_specs=[
            pl.BlockSpec((tm, k), lambda i, j: (i0 + i, 0)),
            pl.BlockSpec((k, tn), lambda i, j: (0, j)),
            pl.BlockSpec((tm, tn), lambda i, j: (i0 + i, j)),
            pl.BlockSpec((None, None, 1, tn), lambda i, j: (((i0 + i) * tm) // group_rows, gate_k, 0, j)),
        ],
        out_specs=pl.BlockSpec((tm, tn), lambda i, j: (i, j)),
        out_shape=jax.ShapeDtypeStruct((n_rows, d), F32),
        compiler_params=_cparams("arbitrary", "arbitrary"),
        name=name,
    )(a, w, x, mod)


def _ffn_up_kernel(h_ref, wg_ref, wu_ref, o_ref):
    h = h_ref[...]
    o_ref[...] = (_silu(_dot(h, wg_ref[...])) * _dot(h, wu_ref[...])).astype(BF16)


def _ffn_up_call(h, w_gate, w_up, tm, tn):
    m, d = h.shape
    n = w_gate.shape[1]
    return pl.pallas_call(
        _ffn_up_kernel,
        grid=(m // tm, n // tn),
        in_specs=[
            pl.BlockSpec((tm, d), lambda i, j: (i, 0)),
            pl.BlockSpec((d, tn), lambda i, j: (0, j)),
            pl.BlockSpec((d, tn), lambda i, j: (0, j)),
        ],
        out_specs=pl.BlockSpec((tm, tn), lambda i, j: (i, j)),
        out_shape=jax.ShapeDtypeStruct((m, n), BF16),
        compiler_params=_cparams("arbitrary", "arbitrary"),
        name="ffn_up",
    )(h, w_gate, w_up)


def _gmlp_kernel(u_ref, v_ref, gain_ref, ws_ref, bs_ref, o_ref):
    gain = gain_ref[...]
    for c in range(u_ref.shape[0] // A_CHUNK):
        rows = slice(c * A_CHUNK, (c + 1) * A_CHUNK)
        v = _gelu_tanh(v_ref[rows, :])
        ms = jnp.mean(v * v, axis=-1, keepdims=True)
        v = ((v * lax.rsqrt(ms + EPS)) * gain).astype(BF16)
        u = _gelu_tanh(u_ref[rows, :])
        for g in range(A_GROUPS):
            cols = slice(g * A_CH, (g + 1) * A_CH)
            mixed = _dot(ws_ref[g], v[:, cols]) + bs_ref[:, g:g + 1]
            o_ref[rows, cols] = (u[:, cols] * mixed).astype(BF16)


def _gmlp_call(proj, v_gain, w_s, b_s, tm=512):
    m = proj.shape[0]
    return pl.pallas_call(
        _gmlp_kernel,
        grid=(m // tm,),
        in_specs=[
            pl.BlockSpec((tm, A_WIDTH), lambda i: (i, COL_U // A_WIDTH)),
            pl.BlockSpec((tm, A_WIDTH), lambda i: (i, COL_V // A_WIDTH)),
            pl.BlockSpec((1, A_WIDTH), lambda i: (0, 0)),
            pl.BlockSpec((A_GROUPS, A_CHUNK, A_CHUNK), lambda i: (0, 0, 0)),
            pl.BlockSpec((A_CHUNK, A_GROUPS), lambda i: (0, 0)),
        ],
        out_specs=pl.BlockSpec((tm, A_WIDTH), lambda i: (i, 0)),
        out_shape=jax.ShapeDtypeStruct((m, A_WIDTH), BF16),
        compiler_params=_cparams("arbitrary"),
        name="gmlp",
    )(proj, proj, v_gain.reshape(1, A_WIDTH), w_s.astype(BF16), b_s.T)


def _conv_prep_kernel(x_ref, w_ref, o_ref, pad_ref, *, seq, rows_per_step):
    halo = 8
    k_half = B_CONV // 2
    pad_ref[0:halo, :] = jnp.zeros((halo, B_DK), F32)
    pad_ref[halo + seq:2 * halo + seq, :] = jnp.zeros((halo, B_DK), F32)
    pad_ref[halo:halo + seq, :] = x_ref[...]
    kind = pl.program_id(1) // B_HEADS
    q_scale = jnp.where(kind == 0, B_DK ** -0.5, 1.0).astype(F32)
    w = w_ref[...]
    for r0 in range(0, seq, rows_per_step):
        acc = jnp.zeros((rows_per_step, B_DK), F32)
        for j in range(B_CONV):
            start = halo + r0 + j - k_half
            acc = acc + pad_ref[start:start + rows_per_step, :] * w[j:j + 1, :]
        y = _silu(acc)
        ss = jnp.sum(y * y, axis=-1, keepdims=True)
        factor = jnp.where(kind == 2, 1.0, lax.rsqrt(ss + EPS) * q_scale)
        o_ref[r0:r0 + rows_per_step, :] = y * factor


def _conv_prep_call(proj, conv_w, seq, n_seq, row_block0):
    nblk = B_QKV // B_DK
    kern = functools.partial(_conv_prep_kernel, seq=seq, rows_per_step=min(seq, 256))
    return pl.pallas_call(
        kern,
        grid=(n_seq, nblk),
        in_specs=[
            pl.BlockSpec((seq, B_DK), lambda s, c: (row_block0 + s, COL_QKV // B_DK + c)),
            pl.BlockSpec((B_CONV, B_DK), lambda s, c: (0, c)),
        ],
        out_specs=pl.BlockSpec((seq, B_DK), lambda s, c: (s, c)),
        out_shape=jax.ShapeDtypeStruct((n_seq * seq, B_QKV), F32),
        scratch_shapes=[pltpu.VMEM((seq + 16, B_DK), F32)],
        compiler_params=_cparams("arbitrary", "arbitrary"),
        name="conv_prep",
    )(proj, conv_w)


def _split3(x):
    hi = x.astype(BF16)
    r1 = x - hi.astype(F32)
    mid = r1.astype(BF16)
    lo = (r1 - mid.astype(F32)).astype(BF16)
    return hi, mid, lo


def _gate_prep_kernel(ab_ref, alog_ref, dt_ref, o_ref):
    t = ab_ref.shape[0]
    x = ab_ref[:, 0:AB_LANES]
    lane = lax.broadcasted_iota(jnp.int32, (t, AB_LANES), 1)
    z = x + dt_ref[...]
    softplus = jnp.maximum(z, 0.0) + jnp.log(1.0 + jnp.exp(-jnp.abs(z)))
    g = -jnp.exp(alog_ref[...]) * softplus
    beta = _sigmoid(x)
    ri = lax.broadcasted_iota(jnp.int32, (t, t), 0)
    ci = lax.broadcasted_iota(jnp.int32, (t, t), 1)
    same = (ri // B_CHUNK) == (ci // B_CHUNK)
    tril = jnp.where(same & (ci <= ri), 1.0, 0.0).astype(BF16)
    triu = jnp.where(same & (ci >= ri), 1.0, 0.0).astype(BF16)
    hi, mid, lo = _split3(g)
    cum_f = _dot(tril, hi) + _dot(tril, mid) + _dot(tril, lo)
    cum_b = _dot(triu, hi) + _dot(triu, mid) + _dot(triu, lo)
    res = jnp.where(lane < B_HEADS, cum_f, jnp.where(lane < 2 * B_HEADS, cum_b, beta))
    for h in range(B_HEADS):
        o_ref[h] = jnp.concatenate([res[:, h + B_HEADS * c:h + B_HEADS * c + 1] for c in range(4)], axis=1)


def _gate_prep_call(proj, a_log, dt_bias, tm=512):
    m = proj.shape[0]
    lane_pad = jnp.zeros((AB_LANES - 2 * B_HEADS,), F32)
    alog_row = jnp.concatenate([a_log.reshape(-1), lane_pad]).reshape(1, AB_LANES)
    dt_row = jnp.concatenate([dt_bias.reshape(-1), lane_pad]).reshape(1, AB_LANES)
    return pl.pallas_call(
        _gate_prep_kernel,
        grid=(m // tm,),
        in_specs=[
            pl.BlockSpec((tm, AB_PAD), lambda i: (i, COL_AB // AB_PAD)),
            pl.BlockSpec((1, AB_LANES), lambda i: (0, 0)),
            pl.BlockSpec((1, AB_LANES), lambda i: (0, 0)),
        ],
        out_specs=pl.BlockSpec((B_HEADS, tm, 4), lambda i: (0, i, 0)),
        out_shape=jax.ShapeDtypeStruct((B_HEADS, m, 4), F32),
        compiler_params=_cparams("arbitrary"),
        name="gate_prep",
    )(proj, alog_row, dt_row)


def _aligned_rows(block, size):
    if isinstance(block, int):
        return slice(block * size, (block + 1) * size)
    return pl.ds(pl.multiple_of(block * size, size), size)


def _delta_kernel(*refs, seq, zero_init, emit_state):
    it = iter(refs)
    q_ref, k_ref, v_ref, gb_ref, gbr_ref, gate_ref, og_ref = (next(it) for _ in range(7))
    s0_refs = None if zero_init else (next(it), next(it))
    y_ref = next(it)
    s_out = (next(it), next(it)) if emit_state else None
    scr = [tuple(next(it) for _ in range(3)) for _ in range(2)]

    c = B_CHUNK
    r = DELTA_ROWS
    n_chunks = seq // c
    ri = lax.broadcasted_iota(jnp.int32, (r, r), 0)
    ci = lax.broadcasted_iota(jnp.int32, (r, r), 1)
    same = (ri >> B_CHUNK_LOG2) == (ci >> B_CHUNK_LOG2)
    masks = ((same & (ri >= ci), same & (ri > ci)), (same & (ri <= ci), same & (ri < ci)))
    col_chunk = lax.broadcasted_iota(jnp.int32, (B_DK, r), 1) >> B_CHUNK_LOG2

    def load_group(gi):
        rows = _aligned_rows(gi, r)
        q = q_ref[rows, :]
        k = k_ref[rows, :]
        kb = k.astype(BF16)
        kq = _dot_nt(jnp.concatenate([kb, q.astype(BF16)], axis=0), kb)
        return dict(gi=gi, rows=rows, q=q, k=k, v=v_ref[rows, :],
                    gbv=gb_ref[rows, :],
                    gbr=gbr_ref[gi],
                    kk=kq[0:r, :], qk_raw=kq[r:2 * r, :])

    def chain_start(g, d):
        incl, strict = masks[d]
        gc_col = g["gbv"][:, d:d + 1]
        beta_col = g["gbv"][:, 2 + d:3 + d]
        gc = jnp.broadcast_to(gc_col, (r, B_DK))
        beta = jnp.broadcast_to(beta_col, (r, B_DK))
        diff = jnp.broadcast_to(gc_col, (r, r)) - jnp.broadcast_to(g["gbr"][d:d + 1, :], (r, r))
        decay = jnp.where(incl, jnp.exp(jnp.where(incl, diff, 0.0)), 0.0)
        lmat = jnp.where(strict, jnp.broadcast_to(beta_col, (r, r)) * g["kk"] * decay, 0.0)
        gam = jnp.exp(gc)
        sol = jnp.concatenate([(beta * gam) * g["k"], beta * g["v"]], axis=1)
        pb = (-lmat).astype(BF16)
        return dict(g=g, d=d, gc=gc, gam=gam, decay=decay, pb=pb, sol=sol + _dot(pb, sol.astype(BF16)))

    def chain_level(ch):
        ch["pb"] = _dot(ch["pb"], ch["pb"]).astype(BF16)
        ch["sol"] = ch["sol"] + _dot(ch["pb"], ch["sol"].astype(BF16))

    def chain_finish(ch):
        g, d, gc, gam, decay = ch["g"], ch["d"], ch["gc"], ch["gam"], ch["decay"]
        gi, rows, q, k, qk_raw = g["gi"], g["rows"], g["q"], g["k"], g["qk_raw"]
        sol = ch["sol"].astype(BF16)
        end_row = c - 1 if d == 0 else 0
        g_end = jnp.concatenate(
            [jnp.broadcast_to(gc[j * c + end_row:j * c + end_row + 1, :], (c, B_DK)) for j in range(DELTA_GROUP)],
            axis=0)
        kd = jnp.exp(g_end - gc) * k
        kd_t = kd.T
        lhs = jnp.concatenate([qk_raw * decay] + [jnp.where(col_chunk == j, kd_t, 0.0)
                                                  for j in range(DELTA_GROUP)], axis=0)
        res = _dot(lhs.astype(BF16), sol)
        qo = res[0:r, :]
        ab = res[r:, :]
        q_prime = (gam * q - qo[:, 0:B_DK]).astype(BF16)
        aq_s, b_s, o_s = scr[d]
        for j in range(DELTA_GROUP):
            n = gi * DELTA_GROUP + j
            aq_s[n, 0:B_DK, :] = (-ab[j * B_DK:(j + 1) * B_DK, 0:B_DK]).astype(BF16)
            aq_s[n, B_DK:B_DK + c, :] = q_prime[j * c:(j + 1) * c, :]
            b_s[n] = ab[j * B_DK:(j + 1) * B_DK, B_DK:B_DK + B_DV]
        o_s[rows, :] = qo[:, B_DK:B_DK + B_DV]

    def step(n, s, d):
        aq_s, b_s, o_s = scr[d]
        rows = _aligned_rows(n, c)
        res = _dot(aq_s[n], s.astype(BF16))
        o_s[rows, :] = o_s[rows, :] + res[B_DK:B_DK + c, :]
        end_row = c - 1 if d == 0 else 0
        g_end = gb_ref[pl.ds(n * c + end_row, 1), :][:, d:d + 1]
        return jnp.exp(g_end) * s + res[0:B_DK, :] + b_s[n]

    n_groups = seq // r
    groups_per_iter = 2 if n_groups % 2 == 0 else 1

    def prep_body(i, carry):
        chains = [chain_start(g, d)
                  for g in [load_group(i * groups_per_iter + j) for j in range(groups_per_iter)]
                  for d in range(2)]
        for _ in range(B_CHUNK_LOG2 - 1):
            for ch in chains:
                chain_level(ch)
        for ch in chains:
            chain_finish(ch)
        return carry

    lax.fori_loop(0, n_groups // groups_per_iter, prep_body, 0)

    def scan_body(i, carry):
        s_f, s_b = carry
        return step(i, s_f, 0), step(n_chunks - 1 - i, s_b, 1)

    if zero_init:
        init = (jnp.zeros((B_DK, B_DV), F32), jnp.zeros((B_DK, B_DV), F32))
    else:
        init = (s0_refs[0][...], s0_refs[1][...])
    s_f, s_b = lax.fori_loop(0, n_chunks, scan_body, init, unroll=2)
    if emit_state:
        s_out[0][...] = s_f
        s_out[1][...] = s_b

    og = og_ref[...]
    o_f, o_b = scr[0][2], scr[1][2]
    rstep = min(seq, 256)
    for r0 in range(0, seq, rstep):
        rows = slice(r0, r0 + rstep)
        o = o_f[rows, :] + o_b[rows, :]
        ms = jnp.mean(o * o, axis=-1, keepdims=True)
        y_ref[rows, :] = (((o * lax.rsqrt(ms + EPS)) * og) * _silu(gate_ref[rows, :])).astype(BF16)


def _delta_call(qkv, gb, gbr, proj, o_gain, states, seq, n_seq, row_block0, layer):
    zero_init = states is None
    emit_state = states is None
    kern = functools.partial(_delta_kernel, seq=seq, zero_init=zero_init, emit_state=emit_state)
    n_grp = seq // DELTA_ROWS
    in_specs = [
        pl.BlockSpec((seq, B_DK), lambda s, h: (s, h)),
        pl.BlockSpec((seq, B_DK), lambda s, h: (s, B_HEADS + h)),
        pl.BlockSpec((seq, B_DV), lambda s, h: (s, 2 * B_HEADS + h)),
        pl.BlockSpec((None, seq, 4), lambda s, h: (h, row_block0 + s, 0)),
        pl.BlockSpec((None, n_grp, 4, DELTA_ROWS), lambda s, h: (h, row_block0 + s, 0, 0)),
        pl.BlockSpec((seq, B_DV), lambda s, h: (row_block0 + s, COL_GATE_B // B_DV + h)),
        pl.BlockSpec((1, B_DV), lambda s, h: (0, 0)),
    ]
    args = [qkv, qkv, qkv, gb, gbr, proj, o_gain.reshape(1, B_DV)]
    if not zero_init:
        st_spec = pl.BlockSpec((None, None, None, B_DK, B_DV), lambda s, h: (s, layer, h, 0, 0))
        in_specs += [st_spec, st_spec]
        args += [states[0], states[1]]
    out_specs = [pl.BlockSpec((seq, B_DV), lambda s, h: (s, h))]
    out_shape = [jax.ShapeDtypeStruct((n_seq * seq, B_VW), BF16)]
    if emit_state:
        so_spec = pl.BlockSpec((None, None, B_DK, B_DV), lambda s, h: (s, h, 0, 0))
        out_specs += [so_spec, so_spec]
        out_shape += [jax.ShapeDtypeStruct((n_seq, B_HEADS, B_DK, B_DV), F32)] * 2
    n_chunks = seq // B_CHUNK
    per_dir = [
        pltpu.VMEM((n_chunks, B_DK + B_CHUNK, B_DK), BF16),
        pltpu.VMEM((n_chunks, B_DK, B_DV), F32),
        pltpu.VMEM((seq, B_DV), F32),
    ]
    return pl.pallas_call(
        kern,
        grid=(n_seq, B_HEADS),
        in_specs=in_specs,
        out_specs=out_specs,
        out_shape=out_shape,
        scratch_shapes=per_dir + per_dir,
        compiler_params=_cparams("arbitrary", "arbitrary"),
        name="delta",
    )(*args)


def _rope_tables(n_tokens):
    rows = n_tokens // GRID_W
    row = jnp.repeat(jnp.arange(rows, dtype=F32), GRID_W)
    col = jnp.tile(jnp.arange(GRID_W, dtype=F32), rows)
    axis_dim = HEAD_DIM // 2
    inv = 1.0 / (ROPE_THETA ** (jnp.arange(0, axis_dim, 2, dtype=F32) / axis_dim))
    ang_r = row[:, None] * inv
    ang_c = col[:, None] * inv
    cos = jnp.concatenate([jnp.cos(ang_r), jnp.cos(ang_r), jnp.cos(ang_c), jnp.cos(ang_c)], axis=1)
    sin = jnp.concatenate([-jnp.sin(ang_r), jnp.sin(ang_r), -jnp.sin(ang_c), jnp.sin(ang_c)], axis=1)
    return cos, sin


def _rope(x, cos, sin):
    quarter = HEAD_DIM // 4
    lane = lax.broadcasted_iota(jnp.int32, x.shape, 1)
    first = (lane % (2 * quarter)) < quarter
    partner = jnp.where(first, pltpu.roll(x, HEAD_DIM - quarter, 1), pltpu.roll(x, quarter, 1))
    return x * cos + partner * sin


def _rms_gain(x, gain):
    ms = jnp.mean(x * x, axis=-1, keepdims=True)
    return (x * lax.rsqrt(ms + EPS)) * gain


def _q_prep_kernel(q_ref, cos_ref, sin_ref, qg_ref, o_ref, *, n_ctx_tiles):
    qg = qg_ref[...]

    def run(use_rope):
        for h in range(C_HEADS):
            cols = slice(h * HEAD_DIM, (h + 1) * HEAD_DIM)
            qh = _rms_gain(q_ref[:, cols], qg)
            if use_rope:
                qh = _rope(qh, cos_ref[...], sin_ref[...])
            o_ref[:, cols] = (qh * (HEAD_DIM ** -0.5)).astype(BF16)

    is_ctx = pl.program_id(0) < n_ctx_tiles

    @pl.when(is_ctx)
    def _():
        run(False)

    @pl.when(jnp.logical_not(is_ctx))
    def _():
        run(True)


def _q_prep_call(proj, q_gain, tables, n_ctx_rows, seq_rows, tm=512):
    m = proj.shape[0]
    tiles_per_seq = seq_rows // tm
    t_spec = pl.BlockSpec((tm, HEAD_DIM), lambda i: (i % tiles_per_seq, 0))
    return pl.pallas_call(
        functools.partial(_q_prep_kernel, n_ctx_tiles=n_ctx_rows // tm),
        grid=(m // tm,),
        in_specs=[
            pl.BlockSpec((tm, C_QW), lambda i: (i, COL_Q // C_QW)),
            t_spec,
            t_spec,
            pl.BlockSpec((1, HEAD_DIM), lambda i: (0, 0)),
        ],
        out_specs=pl.BlockSpec((tm, C_QW), lambda i: (i, 0)),
        out_shape=jax.ShapeDtypeStruct((m, C_QW), BF16),
        compiler_params=_cparams("arbitrary"),
        name="q_prep",
    )(proj, tables[0], tables[1], q_gain.reshape(1, HEAD_DIM))


def _attn_kernel(*refs, seq, n_ctx, rope, tq, emit_k):
    it = iter(refs)
    q_ref, k_ref, v_ref = next(it), next(it), next(it)
    ck_ref, cv_ref = (next(it), next(it)) if n_ctx else (None, None)
    cos_ref, sin_ref = (next(it), next(it)) if rope else (None, None)
    kg_ref = next(it)
    o_ref = next(it)
    knew_ref = next(it) if emit_k else None
    kt_buf, vbuf, s_even, s_odd, m_even, m_odd = (next(it) for _ in range(6))

    n_keys = seq + n_ctx
    step = pl.program_id(2)
    kstep = min(seq, 512)

    @pl.when(step == 0)
    def _():
        s_odd[...] = jnp.zeros(s_odd.shape, F32)
        m_odd[...] = jnp.zeros(m_odd.shape, F32)
        kg = kg_ref[...]
        for r0 in range(0, seq, kstep):
            rows = slice(r0, r0 + kstep)
            kn = _rms_gain(k_ref[rows, :], kg)
            if emit_k:
                knew_ref[rows, :] = kn
            if rope:
                kn = _rope(kn, cos_ref[rows, :], sin_ref[rows, :])
            kt_buf[:, rows] = kn.T.astype(BF16)
            vbuf[rows, 0:HEAD_DIM] = v_ref[rows, :].astype(BF16)
        if n_ctx:
            kt_buf[:, seq:n_keys] = ck_ref[...].T.astype(BF16)
            vbuf[seq:n_keys, 0:HEAD_DIM] = cv_ref[...].astype(BF16)
        vbuf[:, HEAD_DIM:2 * HEAD_DIM] = jnp.ones((n_keys, HEAD_DIM), BF16)

    m_rows = C_GROUP * tq
    q4 = jnp.concatenate([q_ref[:, g * HEAD_DIM:(g + 1) * HEAD_DIM] for g in range(C_GROUP)], axis=0)

    blocks = []
    start = 0
    while start < n_keys:
        size = min(KEY_BLOCK, n_keys - start)
        blocks.append((start, size))
        start += size

    def score_and_finish(s_new, m_new, s_old, m_old):
        m_prev = m_old[...]
        m_lane = jnp.full((m_rows, 128), -jnp.inf, F32)
        acc = jnp.zeros((m_rows, 2 * HEAD_DIM), F32)
        for (b0, bs) in blocks:
            s = _dot(q4, kt_buf[:, b0:b0 + bs])
            s_new[:, b0:b0 + bs] = s
            for j in range(bs // 128):
                m_lane = jnp.maximum(m_lane, s[:, j * 128:(j + 1) * 128])
            shift = jnp.concatenate([m_prev] * (bs // 128), axis=1)
            p = jnp.exp(s_old[:, b0:b0 + bs] - shift).astype(BF16)
            acc = acc + _dot(p, vbuf[b0:b0 + bs, :])
        m_new[...] = jnp.broadcast_to(jnp.max(m_lane, axis=-1, keepdims=True), (m_rows, 128))
        o = acc[:, 0:HEAD_DIM] / acc[:, HEAD_DIM:2 * HEAD_DIM]
        for g in range(C_GROUP):
            o_ref[:, g * HEAD_DIM:(g + 1) * HEAD_DIM] = o[g * tq:(g + 1) * tq, :].astype(BF16)

    @pl.when(step % 2 == 0)
    def _():
        score_and_finish(s_even, m_even, s_odd, m_odd)

    @pl.when(step % 2 == 1)
    def _():
        score_and_finish(s_odd, m_odd, s_even, m_even)


def _attn_call(proj, q_prepared, k_gain, seq, n_seq, row_block0, tq, cache=None, layer=0, tables=None):
    n_ctx = 0 if cache is None else cache[0].shape[2]
    rope = tables is not None
    emit_k = cache is None
    qw = C_GROUP * HEAD_DIM
    nq = seq // tq
    kern = functools.partial(_attn_kernel, seq=seq, n_ctx=n_ctx, rope=rope, tq=tq, emit_k=emit_k)
    in_specs = [
        pl.BlockSpec((tq, qw), lambda b, h, i: ((row_block0 + b) * nq + jnp.minimum(i, nq - 1), h)),
        pl.BlockSpec((seq, HEAD_DIM), lambda b, h, i: (row_block0 + b, COL_K // HEAD_DIM + h)),
        pl.BlockSpec((seq, HEAD_DIM), lambda b, h, i: (row_block0 + b, COL_VC // HEAD_DIM + h)),
    ]
    args = [q_prepared, proj, proj]
    if n_ctx:
        c_spec = pl.BlockSpec((None, None, n_ctx, HEAD_DIM), lambda b, h, i: (b, layer, 0, h))
        in_specs += [c_spec, c_spec]
        args += [cache[0], cache[1]]
    if rope:
        t_spec = pl.BlockSpec((seq, HEAD_DIM), lambda b, h, i: (0, 0))
        in_specs += [t_spec, t_spec]
        args += [tables[0], tables[1]]
    in_specs.append(pl.BlockSpec((1, HEAD_DIM), lambda b, h, i: (0, 0)))
    args.append(k_gain.reshape(1, HEAD_DIM))
    out_specs = [pl.BlockSpec((tq, qw), lambda b, h, i: (b * nq + jnp.maximum(i - 1, 0), h))]
    out_shape = [jax.ShapeDtypeStruct((n_seq * seq, C_QW), BF16)]
    if emit_k:
        out_specs.append(pl.BlockSpec((seq, HEAD_DIM), lambda b, h, i: (b, h)))
        out_shape.append(jax.ShapeDtypeStruct((n_seq * seq, C_KVW), F32))
    n_keys = seq + n_ctx
    return pl.pallas_call(
        kern,
        grid=(n_seq, C_KV, nq + 1),
        in_specs=in_specs,
        out_specs=out_specs,
        out_shape=out_shape,
        scratch_shapes=[
            pltpu.VMEM((HEAD_DIM, n_keys), BF16),
            pltpu.VMEM((n_keys, 2 * HEAD_DIM), BF16),
            pltpu.VMEM((C_GROUP * tq, n_keys), F32),
            pltpu.VMEM((C_GROUP * tq, n_keys), F32),
            pltpu.VMEM((C_GROUP * tq, 128), F32),
            pltpu.VMEM((C_GROUP * tq, 128), F32),
        ],
        compiler_params=_cparams("arbitrary", "arbitrary", "arbitrary"),
        name="attn",
    )(*args)


def _tile(table, layer, m_limit, n_limit):
    tm, tn = table[layer % len(table)]
    return min(tm, m_limit), min(tn, n_limit)


def kernel(x_prompt, x_sample, cache_k, cache_v, state_fwd, state_bwd, c, c_ctx, w_ada, b_ada, w_in, a_v_gain,
           a_w_s, a_b_s, b_conv, b_a_log, b_dt_bias, b_o_gain, c_q_gain, c_k_gain, w_mg, w_br_a, w_br_b, w_br_c,
           w_o, w_gate, w_up, w_down):
    batch, seq, d = x_prompt.shape
    dec_batch, dec_seq, _ = x_sample.shape
    depth = w_in.shape[0]
    n_ctx_rows = batch * seq
    n_lat_rows = dec_batch * dec_seq
    group_rows = dec_seq
    assert n_ctx_rows == group_rows, "context tokens must fill exactly one modulation group"
    n_groups = 1 + dec_batch
    n_tokens = n_ctx_rows + n_lat_rows
    past_len = cache_k.shape[2]

    x = jnp.concatenate([x_prompt.reshape(n_ctx_rows, d), x_sample.reshape(n_lat_rows, d)], axis=0)

    cond_rows = -(-n_groups // 8) * 8
    cond = jnp.concatenate([c_ctx[None, :], c, jnp.zeros((cond_rows - n_groups, d), F32)], axis=0)
    mod_all = _ada_call(cond, w_ada, b_ada)[:, :n_groups].reshape(depth, n_groups, 6, 1, d)

    n_in = w_in.shape[2]
    ab0 = COL_GATE_B + B_VW
    cache_k2 = cache_k.reshape(dec_batch, depth, past_len, C_KVW)
    cache_v2 = cache_v.reshape(dec_batch, depth, past_len, C_KVW)
    tables = _rope_tables(dec_seq)

    new_k, new_v, new_sf, new_sb = [], [], [], []
    y_prompt = y_sample = None
    for l in range(depth):
        mod = mod_all[l]
        w_in_l = jnp.concatenate(
            [w_in[l][:, :ab0], w_in[l][:, ab0 + N_AB:n_in], w_in[l][:, ab0:ab0 + N_AB],
             jnp.zeros((d, AB_PAD - N_AB), F32)], axis=1).astype(BF16)
        proj = _inproj_call(_norm_call(x, mod, 0, 1, group_rows), w_in_l,
                            *_tile(TILES_INPROJ, l, group_rows, N_PROJ))

        y_a = _gmlp_call(proj, a_v_gain[l], a_w_s[l], a_b_s[l])

        gb = _gate_prep_call(proj, b_a_log[l], b_dt_bias[l])
        gbr = gb.reshape(B_HEADS, n_tokens // DELTA_ROWS, DELTA_ROWS, 4).transpose(0, 1, 3, 2)
        qkv_ctx = _conv_prep_call(proj, b_conv[l], seq, batch, 0)
        qkv_lat = _conv_prep_call(proj, b_conv[l], dec_seq, dec_batch, 1)
        yb_ctx, sf, sb = _delta_call(qkv_ctx, gb, gbr, proj, b_o_gain[l], None, seq, batch, 0, l)
        (yb_lat,) = _delta_call(qkv_lat, gb, gbr, proj, b_o_gain[l], (state_fwd, state_bwd), dec_seq, dec_batch,
                                1, l)
        new_sf.append(sf)
        new_sb.append(sb)

        q_prepared = _q_prep_call(proj, c_q_gain[l], tables, n_ctx_rows, dec_seq)
        yc_ctx, k_new = _attn_call(proj, q_prepared, c_k_gain[l], seq, batch, 0, tq=seq)
        (yc_lat,) = _attn_call(proj, q_prepared, c_k_gain[l], dec_seq, dec_batch, 1, tq=128,
                               cache=(cache_k2, cache_v2), layer=l, tables=tables)
        new_k.append(k_new.reshape(batch, seq, C_KV, HEAD_DIM))
        new_v.append(proj[:n_ctx_rows, COL_VC:COL_VC + C_KVW].reshape(batch, seq, C_KV, HEAD_DIM))

        merged = _merge_call(y_a, yb_ctx, yb_lat, yc_ctx, yc_lat, proj, w_mg[l].astype(BF16),
                             w_br_a[l].astype(BF16), w_br_b[l].astype(BF16), w_br_c[l].astype(BF16),
                             *_tile(TILES_MERGE, l, group_rows, d))
        x = _resid_call(merged, w_o[l].astype(BF16), x, mod, 2, group_rows, *_tile(TILES_OUT_PROJ, l, group_rows, d),
                        name="out_proj")
        act = _ffn_up_call(_norm_call(x, mod, 3, 4, group_rows), w_gate[l].astype(BF16), w_up[l].astype(BF16),
                           *_tile(TILES_FFN_UP, l, group_rows, w_gate.shape[2]))
        w_down_l = w_down[l].astype(BF16)
        tm, tn = _tile(TILES_FFN_DOWN, l, group_rows, d)
        if l + 1 < depth:
            x = _resid_call(act, w_down_l, x, mod, 5, group_rows, tm, tn, name="ffn_down")
        else:
            y_prompt = _resid_call(act, w_down_l, x, mod, 5, group_rows, tm, tn, name="ffn_down_ctx",
                                   row0=0, n_rows=n_ctx_rows)
            y_sample = _resid_call(act, w_down_l, x, mod, 5, group_rows, tm, tn, name="ffn_down_lat",
                                   row0=n_ctx_rows, n_rows=n_lat_rows)

    return (y_prompt.reshape(batch, seq, d), y_sample.reshape(dec_batch, dec_seq, d),
            jnp.stack(new_k, axis=1), jnp.stack(new_v, axis=1), jnp.stack(new_sf, axis=1), jnp.stack(new_sb, axis=1))
```
